```python
import jax, jax.numpy as jnp
from jax import lax
import numpy as np

D_MODEL = 2048
BATCH = 4
SEQ = 2048
DEPTH = 1
DEC_BATCH = 128
DEC_SEQ = 8
PAST_LEN = 16384
PAGE_SIZE = 128

N_MEM = 256
CONV_DIM = 1024
CONV_K = 31
DN_HEADS = 16
DN_HEAD_DIM = 128
DN_DIM = DN_HEADS * DN_HEAD_DIM
DN_CONV_K = 4
DN_CHUNK = 64
XA_HEADS = 4
XA_HEAD_DIM = 256
XA_DIM = XA_HEADS * XA_HEAD_DIM
PEER_HEADS = 8
PEER_N_KEYS = 128
PEER_N_EXPERTS = PEER_N_KEYS * PEER_N_KEYS
PEER_DK = 256
PEER_TOPK = 16
PEER_BLOCK = 128
NORM_EPS = 1e-6

IN_SIZES = (CONV_DIM, CONV_DIM, 3 * DN_DIM, DN_DIM, DN_HEADS, DN_HEADS, XA_DIM, D_MODEL, D_MODEL, D_MODEL)
IN_COLS = sum(IN_SIZES)
IN_SPLITS = tuple(sum(IN_SIZES[:i + 1]) for i in range(len(IN_SIZES) - 1))

kernel_name = 'hybrid_conformer_gdn_peer_decode_step'


def rms_norm(x, g):
    xf = x.astype(jnp.float32)
    xf = xf * lax.rsqrt(jnp.mean(xf * xf, axis=-1, keepdims=True) + NORM_EPS)
    return xf.astype(x.dtype) * g


def layer_norm(x, g, b):
    xf = x.astype(jnp.float32)
    mu = jnp.mean(xf, axis=-1, keepdims=True)
    xc = xf - mu
    var = jnp.mean(xc * xc, axis=-1, keepdims=True)
    return (xc * lax.rsqrt(var + NORM_EPS)).astype(x.dtype) * g + b


def l2_normalize(x):
    xf = x.astype(jnp.float32)
    return xf * lax.rsqrt(jnp.sum(xf * xf, axis=-1, keepdims=True) + NORM_EPS)


def causal_depthwise_conv(buf, x, w):
    k = w.shape[0]
    xp = jnp.concatenate([buf.astype(x.dtype), x], axis=1)
    y = lax.conv_general_dilated(xp, w[:, None, :].astype(x.dtype), window_strides=(1,), padding='VALID',
                                 dimension_numbers=('NWC', 'WIO', 'NWC'), feature_group_count=x.shape[-1])
    return y, xp[:, xp.shape[1] - (k - 1):]


def gated_delta_chunked(q, k, v, g, beta, state):
    b, l, h, _ = q.shape
    dv = v.shape[-1]
    c = min(DN_CHUNK, l)
    pad = (-l) % c
    n = (l + pad) // c

    def blocks(t):
        t = jnp.pad(t, [(0, 0), (0, pad)] + [(0, 0)] * (t.ndim - 2))
        t = t.reshape((b, n, c) + t.shape[2:])
        return jnp.moveaxis(t, 3, 1)

    q, k, v, g, beta = blocks(q), blocks(k), blocks(v), blocks(g), blocks(beta)
    gc = jnp.cumsum(g, axis=-1)
    idx = jnp.arange(c)
    incl = idx[:, None] >= idx[None, :]
    strict = idx[:, None] > idx[None, :]
    decay_incl = jnp.exp(jnp.where(incl, gc[..., :, None] - gc[..., None, :], -jnp.inf))
    decay_strict = jnp.where(strict, decay_incl, 0.0)
    kb = k * beta[..., None]
    a_mat = jnp.einsum('bhncd,bhnjd->bhncj', kb, k) * decay_strict
    rhs = jnp.concatenate([v * beta[..., None], kb * jnp.exp(gc)[..., None]], axis=-1)
    sol = lax.linalg.triangular_solve(a_mat + jnp.eye(c, dtype=a_mat.dtype), rhs, left_side=True,
                                      lower=True, unit_diagonal=True)
    value, k_cum = sol[..., :dv], sol[..., dv:]
    qk = jnp.einsum('bhncd,bhnjd->bhncj', q, k) * decay_incl
    q_dec = q * jnp.exp(gc)[..., None]
    k_dec = k * jnp.exp(gc[..., -1:] - gc)[..., None]
    g_last = jnp.exp(gc[..., -1])
    xs = tuple(jnp.moveaxis(t, 2, 0) for t in (value, k_cum, qk, q_dec, k_dec, g_last))

    def step(s, inp):
        val, kc, qk_c, qd, kd, gl = inp
        u = val - jnp.einsum('bhcd,bhde->bhce', kc, s)
        o = jnp.einsum('bhcd,bhde->bhce', qd, s) + jnp.einsum('bhcj,bhje->bhce', qk_c, u)
        s = s * gl[..., None, None] + jnp.einsum('bhcd,bhce->bhde', kd, u)
        return s, o

    state, o = lax.scan(step, state, xs)
    o = jnp.transpose(o, (1, 0, 3, 2, 4)).reshape(b, n * c, h, dv)[:, :l]
    return o, state


def memory_kv(mem, norm_mem, w_mem_kv):
    b, m, _ = mem.shape
    kv = rms_norm(mem, norm_mem) @ w_mem_kv
    k, v = jnp.split(kv, 2, axis=-1)
    return k.reshape(b, m, XA_HEADS, XA_HEAD_DIM), v.reshape(b, m, XA_HEADS, XA_HEAD_DIM)


def memory_attention(q, mem_k, mem_v):
    b, l, _ = q.shape
    q = q.reshape(b, l, XA_HEADS, XA_HEAD_DIM)
    s = jnp.einsum('blhd,bmhd->bhlm', q, mem_k.astype(q.dtype)).astype(jnp.float32) * (XA_HEAD_DIM ** -0.5)
    p = jax.nn.softmax(s, axis=-1).astype(q.dtype)
    o = jnp.einsum('bhlm,bmhd->blhd', p, mem_v.astype(q.dtype))
    return o.reshape(b, l, XA_DIM)


def peer_ffn(h, w_q, keys_1, keys_2, expert_u, expert_v):
    shp = h.shape
    xt = h.reshape(-1, D_MODEL)
    n = xt.shape[0]
    pad = (-n) % PEER_BLOCK
    xb = jnp.pad(xt, ((0, pad), (0, 0))).reshape(-1, PEER_BLOCK, D_MODEL)
    k1 = keys_1.astype(jnp.float32)
    k2 = keys_2.astype(jnp.float32)

    def block(xblk):
        q = (xblk @ w_q).astype(jnp.float32).reshape(PEER_BLOCK, PEER_HEADS, 2, PEER_DK // 2)
        s1 = jnp.einsum('thd,kd->thk', q[:, :, 0], k1)
        s2 = jnp.einsum('thd,kd->thk', q[:, :, 1], k2)
        t1, i1 = lax.top_k(s1, PEER_TOPK)
        t2, i2 = lax.top_k(s2, PEER_TOPK)
        cand = (t1[..., :, None] + t2[..., None, :]).reshape(PEER_BLOCK, PEER_HEADS, PEER_TOPK * PEER_TOPK)
        cidx = (i1[..., :, None] * PEER_N_KEYS + i2[..., None, :]).reshape(PEER_BLOCK, PEER_HEADS, PEER_TOPK * PEER_TOPK)
        top, pos = lax.top_k(cand, PEER_TOPK)
        eidx = jnp.take_along_axis(cidx, pos, axis=-1)
        gate = jax.nn.softmax(top, axis=-1).astype(xblk.dtype)
        u = expert_u[eidx]
        act = jax.nn.gelu(jnp.einsum('td,thkd->thk', xblk, u))
        v = expert_v[eidx]
        return jnp.einsum('thk,thkd->td', gate * act, v)

    out = lax.map(block, xb)
    return out.reshape(-1, D_MODEL)[:n].reshape(shp)


def hybrid_layer(x, mem_k, mem_v, conv_buf, qkv_buf, delta_state,
                 norm_mix, w_in, conv_dw, conv_dw_b, conv_ln_g, conv_ln_b, w_conv_out,
                 dn_conv_w, dn_a_log, dn_dt_bias, dn_norm_g, w_dn_out, w_xa_out, w_out,
                 norm_ffn, w_peer_q, peer_keys_1, peer_keys_2, peer_u, peer_v):
    b, l, _ = x.shape
    h = rms_norm(x, norm_mix)
    proj = h @ w_in
    (glu_a, glu_g, qkv, z, a_raw, b_raw, xq,
     gate_conv, gate_dn, gate_xa) = jnp.split(proj, IN_SPLITS, axis=-1)
    u = glu_a * jax.nn.sigmoid(glu_g)
    c, new_conv_buf = causal_depthwise_conv(conv_buf, u, conv_dw)
    c = jax.nn.silu(layer_norm(c + conv_dw_b, conv_ln_g, conv_ln_b))
    y_conv = c @ w_conv_out
    qkv_c, new_qkv_buf = causal_depthwise_conv(qkv_buf, qkv, dn_conv_w)
    qkv_c = jax.nn.silu(qkv_c)
    q, k, v = (t.reshape(b, l, DN_HEADS, DN_HEAD_DIM) for t in jnp.split(qkv_c, 3, axis=-1))
    q = l2_normalize(q) * (DN_HEAD_DIM ** -0.5)
    k = l2_normalize(k)
    g = -jnp.exp(dn_a_log.astype(jnp.float32)) * jax.nn.softplus(a_raw.astype(jnp.float32) + dn_dt_bias.astype(jnp.float32))
    beta = jax.nn.sigmoid(b_raw.astype(jnp.float32))
    o, new_state = gated_delta_chunked(q, k, v.astype(jnp.float32), g, beta, delta_state.astype(jnp.float32))
    o = rms_norm(o.astype(x.dtype), dn_norm_g) * jax.nn.silu(z.reshape(b, l, DN_HEADS, DN_HEAD_DIM))
    y_dn = o.reshape(b, l, DN_DIM) @ w_dn_out
    y_xa = memory_attention(xq, mem_k, mem_v) @ w_xa_out
    merged = (jax.nn.sigmoid(gate_conv) * y_conv + jax.nn.sigmoid(gate_dn) * y_dn
              + jax.nn.sigmoid(gate_xa) * y_xa)
    x = x + merged @ w_out
    x = x + peer_ffn(rms_norm(x, norm_ffn), w_peer_q, peer_keys_1, peer_keys_2, peer_u, peer_v)
    return x, new_conv_buf, new_qkv_buf, new_state.astype(delta_state.dtype)


def setup_inputs(seed: int = 0) -> dict:
    key = jax.random.key(seed)
    ks = iter(jax.random.split(key, 48))
    f32 = jnp.float32
    L = DEPTH

    def nrm(shape, scale):
        return jax.random.normal(next(ks), shape, f32) * scale

    def gain(shape):
        return 1.0 + nrm(shape, 0.02)

    return {
        'x_prompt': nrm((BATCH, SEQ, D_MODEL), 1.0),
        'x_sample': nrm((DEC_BATCH, DEC_SEQ, D_MODEL), 1.0),
        'mem_prompt': nrm((BATCH, N_MEM, D_MODEL), 1.0),
        'cache_mem_k': nrm((L, DEC_BATCH, N_MEM, XA_HEADS, XA_HEAD_DIM), 1.0),
        'cache_mem_v': nrm((L, DEC_BATCH, N_MEM, XA_HEADS, XA_HEAD_DIM), 1.0),
        'state_conv': nrm((L, DEC_BATCH, CONV_K - 1, CONV_DIM), 1.0),
        'state_qkv_conv': nrm((L, DEC_BATCH, DN_CONV_K - 1, 3 * DN_DIM), 1.0),
        'state_delta': nrm((L, DEC_BATCH, DN_HEADS, DN_HEAD_DIM, DN_HEAD_DIM), 0.1),
        'norm_mix': gain((L, D_MODEL)),
        'norm_mem': gain((L, D_MODEL)),
        'w_in': nrm((L, D_MODEL, IN_COLS), D_MODEL ** -0.5),
        'conv_dw': nrm((L, CONV_K, CONV_DIM), CONV_K ** -0.5),
        'conv_dw_b': nrm((L, CONV_DIM), 0.02),
        'conv_ln_g': gain((L, CONV_DIM)),
        'conv_ln_b': nrm((L, CONV_DIM), 0.02),
        'w_conv_out': nrm((L, CONV_DIM, D_MODEL), CONV_DIM ** -0.5),
        'dn_conv_w': nrm((L, DN_CONV_K, 3 * DN_DIM), DN_CONV_K ** -0.5),
        'dn_a_log': jnp.log(jax.random.uniform(next(ks), (L, DN_HEADS), f32, 1.0, 16.0)),
        'dn_dt_bias': jnp.log(jnp.expm1(jax.random.uniform(next(ks), (L, DN_HEADS), f32, 0.001, 0.1))),
        'dn_norm_g': gain((L, DN_HEAD_DIM)),
        'w_dn_out': nrm((L, DN_DIM, D_MODEL), DN_DIM ** -0.5),
        'w_mem_kv': nrm((L, D_MODEL, 2 * XA_DIM), D_MODEL ** -0.5),
        'w_xa_out': nrm((L, XA_DIM, D_MODEL), XA_DIM ** -0.5),
        'w_out': nrm((L, D_MODEL, D_MODEL), D_MODEL ** -0.5),
        'norm_ffn': gain((L, D_MODEL)),
        'w_peer_q': nrm((L, D_MODEL, PEER_HEADS * PEER_DK), D_MODEL ** -0.5),
        'peer_keys_1': nrm((L, PEER_N_KEYS, PEER_DK // 2), (PEER_DK // 2) ** -0.5),
        'peer_keys_2': nrm((L, PEER_N_KEYS, PEER_DK // 2), (PEER_DK // 2) ** -0.5),
        'peer_u': nrm((L, PEER_N_EXPERTS, D_MODEL), D_MODEL ** -0.5),
        'peer_v': nrm((L, PEER_N_EXPERTS, D_MODEL), PEER_HEADS ** -0.5),
        'norm_final': gain((D_MODEL,)),
    }


def reference(x_prompt, x_sample, mem_prompt, cache_mem_k, cache_mem_v, state_conv, state_qkv_conv, state_delta,
              norm_mix, norm_mem, w_in, conv_dw, conv_dw_b, conv_ln_g, conv_ln_b, w_conv_out,
              dn_conv_w, dn_a_log, dn_dt_bias, dn_norm_g, w_dn_out, w_mem_kv, w_xa_out, w_out,
              norm_ffn, w_peer_q, peer_keys_1, peer_keys_2, peer_u, peer_v, norm_final):
    bp = x_prompt.shape[0]
    dt = x_prompt.dtype
    xp, xs = x_prompt, x_sample
    mk_p, mv_p, conv_p, qkv_p, delta_p, conv_s, qkv_s, delta_s = [], [], [], [], [], [], [], []
    for i in range(DEPTH):
        lw = (norm_mix[i], w_in[i], conv_dw[i], conv_dw_b[i], conv_ln_g[i], conv_ln_b[i], w_conv_out[i],
              dn_conv_w[i], dn_a_log[i], dn_dt_bias[i], dn_norm_g[i], w_dn_out[i], w_xa_out[i], w_out[i],
              norm_ffn[i], w_peer_q[i], peer_keys_1[i], peer_keys_2[i], peer_u[i], peer_v[i])
        mk, mv = memory_kv(mem_prompt, norm_mem[i], w_mem_kv[i])
        xp, cb, qb, st = hybrid_layer(
            xp, mk, mv,
            jnp.zeros((bp, CONV_K - 1, CONV_DIM), dt),
            jnp.zeros((bp, DN_CONV_K - 1, 3 * DN_DIM), dt),
            jnp.zeros((bp, DN_HEADS, DN_HEAD_DIM, DN_HEAD_DIM), jnp.float32),
            *lw)
        mk_p.append(mk)
        mv_p.append(mv)
        conv_p.append(cb)
        qkv_p.append(qb)
        delta_p.append(st)
        xs, cb2, qb2, st2 = hybrid_layer(
            xs, cache_mem_k[i], cache_mem_v[i], state_conv[i], state_qkv_conv[i], state_delta[i], *lw)
        conv_s.append(cb2)
        qkv_s.append(qb2)
        delta_s.append(st2)
    y_prompt = rms_norm(xp, norm_final)
    y_sample = rms_norm(xs, norm_final)
    return (y_prompt, y_sample, jnp.stack(mk_p), jnp.stack(mv_p), jnp.stack(conv_p), jnp.stack(qkv_p),
            jnp.stack(delta_p), jnp.stack(conv_s), jnp.stack(qkv_s), jnp.stack(delta_s))
```

```python
import functools

import jax
import jax.numpy as jnp
from jax import lax
from jax.experimental import pallas as pl
from jax.experimental.pallas import tpu as pltpu

F32 = jnp.float32
BF16 = jnp.bfloat16
HIGHEST = lax.Precision.HIGHEST

NORM_EPS = 1e-6
LANES = 128
SUBLANES = 8
VMEM_LIMIT_BYTES = 56 * 1024 * 1024

CONV_DIM = 1024
CONV_K = 31
DN_HEADS = 16
DN_HEAD_DIM = 128
DN_DIM = DN_HEADS * DN_HEAD_DIM
DN_CONV_K = 4
DN_CHUNK = 64
XA_HEADS = 4
XA_HEAD_DIM = 256
XA_DIM = XA_HEADS * XA_HEAD_DIM
PEER_HEADS = 8
PEER_N_KEYS = 128
PEER_DK = 256
PEER_TOPK = 16

COL_QKV = 0
COL_GLU = 3 * DN_DIM
COL_Z = COL_GLU + 2 * CONV_DIM
COL_GATES = COL_Z + DN_DIM
COL_XQ = COL_GATES + 3 * 2048
COL_AB = COL_XQ + XA_DIM
PROJ_TN = 1280
PROJ_COLS = 14 * PROJ_TN


def _cparams(*dims):
    return pltpu.CompilerParams(dimension_semantics=dims, vmem_limit_bytes=VMEM_LIMIT_BYTES)


def _dot(a, b, precision=None):
    return jnp.dot(a, b, preferred_element_type=F32, precision=precision)


def _dot_nt(a, b):
    return lax.dot_general(a, b, (((1,), (1,)), ((), ())), preferred_element_type=F32)


def _dot_tn(a, b):
    return lax.dot_general(a, b, (((0,), (0,)), ((), ())), preferred_element_type=F32)


def _rms(x, g):
    return x * lax.rsqrt(jnp.mean(x * x, axis=-1, keepdims=True) + NORM_EPS) * g


def _silu(x):
    return x * jax.nn.sigmoid(x)


def _norm_matmul_kernel(x_ref, g_ref, w_ref, o_ref, h_ref):
    @pl.when(pl.program_id(1) == 0)
    def _():
        h_ref[...] = _rms(x_ref[...], g_ref[...]).astype(BF16)

    o_ref[...] = _dot(h_ref[...], w_ref[...])


def _norm_matmul(x, g, w, *, tm, tn, name):
    t, d = x.shape
    n = w.shape[1]
    return pl.pallas_call(
        _norm_matmul_kernel,
        grid=(t // tm, n // tn),
        in_specs=[
            pl.BlockSpec((tm, d), lambda i, j: (i, 0)),
            pl.BlockSpec((1, d), lambda i, j: (0, 0)),
            pl.BlockSpec((d, tn), lambda i, j: (0, j)),
        ],
        out_specs=pl.BlockSpec((tm, tn), lambda i, j: (i, j)),
        out_shape=jax.ShapeDtypeStruct((t, n), F32),
        scratch_shapes=[pltpu.VMEM((tm, d), BF16)],
        compiler_params=_cparams("parallel", "arbitrary"),
        name=name,
    )(x, g.reshape(1, d), w)


def _conv_kernel(a_ref, g_ref, buf_ref, w_ref, b_ref, lng_ref, lnb_ref, c_ref, nbuf_ref, xp_ref, y_ref, *, tt, nt):
    i = pl.program_id(1)
    head = 32
    off = head - (CONV_K - 1)

    @pl.when(i == 0)
    def _():
        xp_ref[0:off, :] = jnp.zeros((off, CONV_DIM), F32)
        xp_ref[off:head, :] = buf_ref[0]

    xp_ref[head:head + tt, :] = a_ref[...] * jax.nn.sigmoid(g_ref[...])

    rt = min(32, tt)
    cw = 512
    for r in range(tt // rt):
        for cb in range(CONV_DIM // cw):
            cs = slice(cb * cw, (cb + 1) * cw)
            acc = jnp.zeros((rt, cw), F32)
            for j in range(CONV_K):
                acc = acc + w_ref[j:j + 1, cs] * xp_ref[pl.ds(r * rt + off + j, rt), cs]
            y_ref[r * rt:(r + 1) * rt, cs] = acc + b_ref[:, cs]

    y = y_ref[...]
    mu = jnp.mean(y, axis=-1, keepdims=True)
    yc = y - mu
    var = jnp.mean(yc * yc, axis=-1, keepdims=True)
    c_ref[...] = _silu(yc * lax.rsqrt(var + NORM_EPS) * lng_ref[...] + lnb_ref[...])

    tail = xp_ref[tt:tt + head, :]

    @pl.when(i == nt - 1)
    def _():
        nbuf_ref[0] = tail[off:, :]

    xp_ref[0:head, :] = tail


def _conv_branch(proj, buf, w, b, lng, lnb, *, bsz, seq, tt):
    nt = seq // tt
    ca = COL_GLU // CONV_DIM
    row = lambda b_, i: b_ * nt + i
    vec = lambda v: v.reshape(1, CONV_DIM)
    return pl.pallas_call(
        functools.partial(_conv_kernel, tt=tt, nt=nt),
        grid=(bsz, nt),
        in_specs=[
            pl.BlockSpec((tt, CONV_DIM), lambda b_, i: (row(b_, i), ca)),
            pl.BlockSpec((tt, CONV_DIM), lambda b_, i: (row(b_, i), ca + 1)),
            pl.BlockSpec((1, CONV_K - 1, CONV_DIM), lambda b_, i: (b_, 0, 0)),
            pl.BlockSpec((CONV_K, CONV_DIM), lambda b_, i: (0, 0)),
            pl.BlockSpec((1, CONV_DIM), lambda b_, i: (0, 0)),
            pl.BlockSpec((1, CONV_DIM), lambda b_, i: (0, 0)),
            pl.BlockSpec((1, CONV_DIM), lambda b_, i: (0, 0)),
        ],
        out_specs=[
            pl.BlockSpec((tt, CONV_DIM), lambda b_, i: (row(b_, i), 0)),
            pl.BlockSpec((1, CONV_K - 1, CONV_DIM), lambda b_, i: (b_, 0, 0)),
        ],
        out_shape=[
            jax.ShapeDtypeStruct((bsz * seq, CONV_DIM), F32),
            jax.ShapeDtypeStruct((bsz, CONV_K - 1, CONV_DIM), F32),
        ],
        scratch_shapes=[pltpu.VMEM((tt + 32, CONV_DIM), F32), pltpu.VMEM((tt, CONV_DIM), F32)],
        compiler_params=_cparams("parallel", "arbitrary"),
        name="conv_branch",
    )(proj, proj, buf, w, vec(b), vec(lng), vec(lnb))


def _dn_prep_kernel(x_ref, ab_ref, buf_ref, w_ref, alog_ref, dtb_ref, o_ref, g_ref, beta_ref, nbuf_ref, xp_ref, *, tt, nt):
    i = pl.program_id(1)
    head = SUBLANES
    off = head - (DN_CONV_K - 1)

    @pl.when(i == 0)
    def _():
        xp_ref[0:off, :] = jnp.zeros((off, 3 * DN_DIM), F32)
        xp_ref[off:head, :] = buf_ref[0]

    xp_ref[head:head + tt, :] = x_ref[...]

    for blk in range(3 * DN_HEADS):
        cs = slice(blk * DN_HEAD_DIM, (blk + 1) * DN_HEAD_DIM)
        acc = jnp.zeros((tt, DN_HEAD_DIM), F32)
        for j in range(DN_CONV_K):
            acc = acc + w_ref[j:j + 1, cs] * xp_ref[pl.ds(off + j, tt), cs]
        y = _silu(acc)
        if blk < 2 * DN_HEADS:
            y = y * lax.rsqrt(jnp.sum(y * y, axis=-1, keepdims=True) + NORM_EPS)
            if blk < DN_HEADS:
                y = y * (DN_HEAD_DIM ** -0.5)
        o_ref[:, cs] = y

    ab = ab_ref[...]
    a_raw = ab[:, 0:DN_HEADS]
    b_raw = ab[:, DN_HEADS:2 * DN_HEADS]
    sp_in = a_raw + dtb_ref[...]
    softplus = jnp.maximum(sp_in, 0.0) + jnp.log1p(jnp.exp(-jnp.abs(sp_in)))
    g_ref[...] = -jnp.exp(alog_ref[...]) * softplus
    beta_ref[...] = jax.nn.sigmoid(b_raw)

    tail = xp_ref[tt:tt + head, :]

    @pl.when(i == nt - 1)
    def _():
        nbuf_ref[0] = tail[off:, :]

    xp_ref[0:head, :] = tail


def _dn_prep(proj, buf, w, a_log, dt_bias, *, bsz, seq, tt):
    nt = seq // tt
    dq = 3 * DN_DIM
    row = lambda b_, i: b_ * nt + i
    return pl.pallas_call(
        functools.partial(_dn_prep_kernel, tt=tt, nt=nt),
        grid=(bsz, nt),
        in_specs=[
            pl.BlockSpec((tt, dq), lambda b_, i: (row(b_, i), COL_QKV // dq)),
            pl.BlockSpec((tt, LANES), lambda b_, i: (row(b_, i), COL_AB // LANES)),
            pl.BlockSpec((1, DN_CONV_K - 1, dq), lambda b_, i: (b_, 0, 0)),
            pl.BlockSpec((DN_CONV_K, dq), lambda b_, i: (0, 0)),
            pl.BlockSpec((1, DN_HEADS), lambda b_, i: (0, 0)),
            pl.BlockSpec((1, DN_HEADS), lambda b_, i: (0, 0)),
        ],
        out_specs=[
            pl.BlockSpec((tt, dq), lambda b_, i: (row(b_, i), 0)),
            pl.BlockSpec((tt, DN_HEADS), lambda b_, i: (row(b_, i), 0)),
            pl.BlockSpec((tt, DN_HEADS), lambda b_, i: (row(b_, i), 0)),
            pl.BlockSpec((1, DN_CONV_K - 1, dq), lambda b_, i: (b_, 0, 0)),
        ],
        out_shape=[
            jax.ShapeDtypeStruct((bsz * seq, dq), F32),
            jax.ShapeDtypeStruct((bsz * seq, DN_HEADS), F32),
            jax.ShapeDtypeStruct((bsz * seq, DN_HEADS), F32),
            jax.ShapeDtypeStruct((bsz, DN_CONV_K - 1, dq), F32),
        ],
        scratch_shapes=[pltpu.VMEM((tt + SUBLANES, dq), F32)],
        compiler_params=_cparams("parallel", "arbitrary"),
        name="dn_prep",
    )(proj, proj, buf, w, a_log.reshape(1, DN_HEADS), dt_bias.reshape(1, DN_HEADS))


def _unit_lower_inverse(a, c):
    ri = lax.broadcasted_iota(jnp.int32, (c, c), 0)
    ci = lax.broadcasted_iota(jnp.int32, (c, c), 1)
    eye = (ri == ci).astype(F32)
    mm = functools.partial(_dot, precision=HIGHEST)
    ad = jnp.where((ri >> 3) == (ci >> 3), a, 0.0)
    a2 = mm(ad, ad)
    a4 = mm(a2, a2)
    x = mm(mm(eye - ad, eye + a2), eye + a4)
    shift = 3
    while (1 << shift) < c:
        off_block = ((ri >> (shift + 1)) == (ci >> (shift + 1))) & ((ri >> shift) != (ci >> shift))
        x = x - mm(mm(x, jnp.where(off_block, a, 0.0)), x)
        shift += 1
    return x


def _dn_chunk_kernel(q_ref, k_ref, v_ref, z_ref, gcol_ref, bcol_ref, grow_ref, s0_ref, ng_ref, o_ref, s_ref, *, c):
    @pl.when(pl.program_id(2) == 0)
    def _():
        s_ref[...] = s0_ref[...]

    q = q_ref[...]
    k = k_ref[...]
    v = v_ref[...]
    gcol = gcol_ref[0, 0]
    beta = bcol_ref[0, 0]
    grow = grow_ref[0, 0, 0]
    ri = lax.broadcasted_iota(jnp.int32, (c, c), 0)
    ci = lax.broadcasted_iota(jnp.int32, (c, c), 1)
    incl = ri >= ci
    gc_col = jnp.sum(jnp.where(incl, grow, 0.0), axis=1, keepdims=True)
    gc_row = jnp.sum(jnp.where(ri <= ci, gcol, 0.0), axis=0, keepdims=True)
    decay_incl = jnp.exp(jnp.where(incl, gc_col - gc_row, -jnp.inf))
    decay_strict = jnp.where(ri > ci, decay_incl, 0.0)

    kb = k * beta
    k16 = k.astype(BF16)
    a_mat = _dot_nt(kb.astype(BF16), k16) * decay_strict
    t_inv = _unit_lower_inverse(a_mat, c)
    e_gc = jnp.exp(gc_col)
    value = _dot(t_inv, v * beta, precision=HIGHEST)
    k_cum = _dot(t_inv, kb * e_gc, precision=HIGHEST)
    qk = _dot_nt(q.astype(BF16), k16) * decay_incl
    gc_last = gc_col[c - 1:c, :]
    k_dec = k * jnp.exp(gc_last - gc_col)

    s = s_ref[0, 0]
    s16 = s.astype(BF16)
    u = value - _dot(k_cum.astype(BF16), s16)
    u16 = u.astype(BF16)
    o = _dot((q * e_gc).astype(BF16), s16) + _dot(qk.astype(BF16), u16)
    s_ref[0, 0] = s * jnp.exp(gc_last) + _dot_tn(k_dec.astype(BF16), u16)

    o_ref[...] = _rms(o, ng_ref[...]) * _silu(z_ref[...])


def _dn_chunk(qkvc, proj, g, beta, s0, norm_g, *, bsz, seq, c):
    n = seq // c
    h_ = DN_HEADS
    gt = jnp.transpose(g.reshape(bsz, seq, h_), (0, 2, 1))
    bt = jnp.transpose(beta.reshape(bsz, seq, h_), (0, 2, 1))
    gcol = gt.reshape(bsz, h_, seq, 1)
    bcol = bt.reshape(bsz, h_, seq, 1)
    grow = gt.reshape(bsz, h_, n, 1, c)
    row = lambda b_, n_: b_ * n + n_
    d = DN_HEAD_DIM
    return pl.pallas_call(
        functools.partial(_dn_chunk_kernel, c=c),
        grid=(bsz, h_, n),
        in_specs=[
            pl.BlockSpec((c, d), lambda b_, hh, n_: (row(b_, n_), hh)),
            pl.BlockSpec((c, d), lambda b_, hh, n_: (row(b_, n_), h_ + hh)),
            pl.BlockSpec((c, d), lambda b_, hh, n_: (row(b_, n_), 2 * h_ + hh)),
            pl.BlockSpec((c, d), lambda b_, hh, n_: (row(b_, n_), COL_Z // d + hh)),
            pl.BlockSpec((1, 1, c, 1), lambda b_, hh, n_: (b_, hh, n_, 0)),
            pl.BlockSpec((1, 1, c, 1), lambda b_, hh, n_: (b_, hh, n_, 0)),
            pl.BlockSpec((1, 1, 1, 1, c), lambda b_, hh, n_: (b_, hh, n_, 0, 0)),
            pl.BlockSpec((1, 1, d, d), lambda b_, hh, n_: (b_, hh, 0, 0)),
            pl.BlockSpec((1, d), lambda b_, hh, n_: (0, 0)),
        ],
        out_specs=[
            pl.BlockSpec((c, d), lambda b_, hh, n_: (row(b_, n_), hh)),
            pl.BlockSpec((1, 1, d, d), lambda b_, hh, n_: (b_, hh, 0, 0)),
        ],
        out_shape=[
            jax.ShapeDtypeStruct((bsz * seq, DN_DIM), F32),
            jax.ShapeDtypeStruct((bsz, h_, d, d), F32),
        ],
        compiler_params=_cparams("parallel", "parallel", "arbitrary"),
        name="dn_chunk",
    )(qkvc, qkvc, qkvc, proj, gcol, bcol, grow, s0, norm_g.reshape(1, d))


def _mem_attn_kernel(q_ref, k_ref, v_ref, o_ref):
    for h in range(XA_HEADS):
        cs = slice(h * XA_HEAD_DIM, (h + 1) * XA_HEAD_DIM)
        s = _dot_nt(q_ref[:, cs].astype(BF16), k_ref[0, :, cs].astype(BF16)) * (XA_HEAD_DIM ** -0.5)
        e = jnp.exp(s - jnp.max(s, axis=-1, keepdims=True))
        p = e / jnp.sum(e, axis=-1, keepdims=True)
        o_ref[:, cs] = _dot(p.astype(BF16), v_ref[0, :, cs].astype(BF16))


def _mem_attention(proj, mem_k, mem_v, *, bsz, seq, tl):
    nl = seq // tl
    m = mem_k.shape[1]
    return pl.pallas_call(
        _mem_attn_kernel,
        grid=(bsz, nl),
        in_specs=[
            pl.BlockSpec((tl, XA_DIM), lambda b_, i: (b_ * nl + i, COL_XQ // XA_DIM)),
            pl.BlockSpec((1, m, XA_DIM), lambda b_, i: (b_, 0, 0)),
            pl.BlockSpec((1, m, XA_DIM), lambda b_, i: (b_, 0, 0)),
        ],
        out_specs=pl.BlockSpec((tl, XA_DIM), lambda b_, i: (b_ * nl + i, 0)),
        out_shape=jax.ShapeDtypeStruct((bsz * seq, XA_DIM), F32),
        compiler_params=_cparams("parallel", "arbitrary"),
        name="mem_attention",
    )(proj, mem_k, mem_v)


def _merge_kernel(c_ref, o_ref, a_ref, gc_ref, gd_ref, gx_ref, wc_ref, wd_ref, wx_ref, m_ref):
    yc = _dot(c_ref[...].astype(BF16), wc_ref[...])
    yd = _dot(o_ref[...].astype(BF16), wd_ref[...])
    yx = _dot(a_ref[...].astype(BF16), wx_ref[...])
    m_ref[...] = (jax.nn.sigmoid(gc_ref[...]) * yc + jax.nn.sigmoid(gd_ref[...]) * yd
                  + jax.nn.sigmoid(gx_ref[...]) * yx).astype(BF16)


def _merge(c, o, a, proj, wc, wd, wx, *, tm, tn):
    t = c.shape[0]
    dm = wc.shape[1]
    gcol = COL_GATES // tn
    gstep = dm // tn
    return pl.pallas_call(
        _merge_kernel,
        grid=(t // tm, dm // tn),
        in_specs=[
            pl.BlockSpec((tm, CONV_DIM), lambda i, j: (i, 0)),
            pl.BlockSpec((tm, DN_DIM), lambda i, j: (i, 0)),
            pl.BlockSpec((tm, XA_DIM), lambda i, j: (i, 0)),
            pl.BlockSpec((tm, tn), lambda i, j: (i, gcol + j)),
            pl.BlockSpec((tm, tn), lambda i, j: (i, gcol + gstep + j)),
            pl.BlockSpec((tm, tn), lambda i, j: (i, gcol + 2 * gstep + j)),
            pl.BlockSpec((CONV_DIM, tn), lambda i, j: (0, j)),
            pl.BlockSpec((DN_DIM, tn), lambda i, j: (0, j)),
            pl.BlockSpec((XA_DIM, tn), lambda i, j: (0, j)),
        ],
        out_specs=pl.BlockSpec((tm, tn), lambda i, j: (i, j)),
        out_shape=jax.ShapeDtypeStruct((t, dm), BF16),
        compiler_params=_cparams("parallel", "arbitrary"),
        name="merge",
    )(c, o, a, proj, proj, proj, wc, wd, wx)


def _out_proj_kernel(m_ref, w_ref, x_ref, o_ref):
    o_ref[...] = x_ref[...] + _dot(m_ref[...], w_ref[...])


def _out_proj(merged, w, x, *, tm, tn):
    t, dm = x.shape
    return pl.pallas_call(
        _out_proj_kernel,
        grid=(t // tm, dm // tn),
        in_specs=[
            pl.BlockSpec((tm, dm), lambda i, j: (i, 0)),
            pl.BlockSpec((dm, tn), lambda i, j: (0, j)),
            pl.BlockSpec((tm, tn), lambda i, j: (i, j)),
        ],
        out_specs=pl.BlockSpec((tm, tn), lambda i, j: (i, j)),
        out_shape=jax.ShapeDtypeStruct((t, dm), F32),
        compiler_params=_cparams("parallel", "arbitrary"),
        name="out_proj",
    )(merged, w, x)


def _top16(s, tm):
    key = lax.broadcasted_iota(jnp.int32, s.shape, 0).astype(F32)
    work = s
    rank = jnp.full(s.shape, float(PEER_TOPK), F32)
    vals = []
    for a in range(PEER_TOPK):
        m = jnp.max(work, axis=0, keepdims=True)
        idx = jnp.min(jnp.where(work == m, key, float(PEER_N_KEYS)), axis=0, keepdims=True)
        sel = key == idx
        rank = jnp.where(sel, float(a), rank)
        work = jnp.where(sel, -jnp.inf, work)
        vals.append(m)
    return vals, rank


def _peer_route_kernel(q_ref, k1_ref, k2_ref, e1_ref, e2_ref, r2_ref, nrow_ref, *, tm):
    half = PEER_DK // 2
    q = q_ref[...]
    s1 = _dot_nt(k1_ref[...].astype(BF16), q[:, :half].astype(BF16))
    s2 = _dot_nt(k2_ref[...].astype(BF16), q[:, half:].astype(BF16))
    t1, rank1 = _top16(s1, tm)
    t2, rank2 = _top16(s2, tm)

    k = PEER_TOPK
    t2s = jnp.concatenate(t2, axis=0)
    sub = lax.broadcasted_iota(jnp.int32, (k, tm), 0).astype(F32)
    work = [t1[a] + t2s for a in range(k)]
    pos = [sub + float(a * k) for a in range(k)]
    tops = []
    for _ in range(k):
        m = work[0]
        for a in range(1, k):
            m = jnp.maximum(m, work[a])
        m = jnp.max(m, axis=0, keepdims=True)
        p = jnp.where(work[0] == m, pos[0], float(k * k))
        for a in range(1, k):
            p = jnp.minimum(p, jnp.where(work[a] == m, pos[a], float(k * k)))
        p = jnp.min(p, axis=0, keepdims=True)
        work = [jnp.where(pos[a] == p, -jnp.inf, work[a]) for a in range(k)]
        tops.append(m)
    zsum = jnp.zeros((1, tm), F32)
    for m in tops:
        zsum = zsum + jnp.exp(m - tops[0])
    nrow = jnp.zeros(s1.shape, F32)
    for a in range(k):
        n_a = jnp.sum((work[a] == -jnp.inf).astype(F32), axis=0, keepdims=True)
        nrow = jnp.where(rank1 == float(a), n_a, nrow)

    e1_ref[0] = jnp.exp(s1 - t1[0]) / zsum
    e2_ref[0] = jnp.exp(s2 - t2[0])
    r2_ref[0] = rank2
    nrow_ref[0] = nrow


def _peer_route(q, k1, k2, *, tm):
    t = q.shape[0]
    nk = PEER_N_KEYS
    out = jax.ShapeDtypeStruct((PEER_HEADS, nk, t), F32)
    spec = pl.BlockSpec((1, nk, tm), lambda i, h: (h, 0, i))
    return pl.pallas_call(
        functools.partial(_peer_route_kernel, tm=tm),
        grid=(t // tm, PEER_HEADS),
        in_specs=[
            pl.BlockSpec((tm, PEER_DK), lambda i, h: (i, h)),
            pl.BlockSpec((nk, PEER_DK // 2), lambda i, h: (0, 0)),
            pl.BlockSpec((nk, PEER_DK // 2), lambda i, h: (0, 0)),
        ],
        out_specs=[spec, spec, spec, spec],
        out_shape=[out, out, out, out],
        compiler_params=_cparams("parallel", "arbitrary"),
        name="peer_route",
    )(q, k1, k2)


def _gelu_tanh(x):
    return 0.5 * x * (1.0 + jnp.tanh(0.7978845608028654 * (x + 0.044715 * (x * x * x))))


def _peer_dense_kernel(x_ref, gn_ref, gf_ref, e1_ref, nrow_ref, e2_ref, r2_ref, u_ref, v_ref, y_ref,
                       h_ref, p_ref, acc_ref, *, tm, eb, nj):
    j = pl.program_id(1)
    nk = PEER_N_KEYS

    @pl.when(j == 0)
    def _():
        h_ref[...] = _rms(x_ref[...], gn_ref[...]).astype(BF16)
        acc_ref[...] = jnp.zeros_like(acc_ref)

    for bi in range(eb):
        for th in range(tm // LANES):
            ts = slice(th * LANES, (th + 1) * LANES)
            s = _dot_nt(u_ref[bi * nk:(bi + 1) * nk, :], h_ref[ts, :])
            w = jnp.zeros((nk, LANES), F32)
            for h in range(PEER_HEADS):
                sel = r2_ref[h, :, ts] < nrow_ref[h, bi:bi + 1, ts]
                w = w + jnp.where(sel, e1_ref[h, bi:bi + 1, ts] * e2_ref[h, :, ts], 0.0)
            p_ref[ts, bi * nk:(bi + 1) * nk] = (w * _gelu_tanh(s)).T.astype(BF16)
    acc_ref[...] += _dot(p_ref[...], v_ref[...])

    @pl.when(j == nj - 1)
    def _():
        y_ref[...] = _rms(x_ref[...] + acc_ref[...], gf_ref[...])


def _peer_dense(x, g_ffn, g_final, e1, e2, r2, nrow, u16, v16, *, tm, eb):
    t, dm = x.shape
    nk = PEER_N_KEYS
    ne = u16.shape[0]
    nj = ne // (eb * nk)
    return pl.pallas_call(
        functools.partial(_peer_dense_kernel, tm=tm, eb=eb, nj=nj),
        grid=(t // tm, nj),
        in_specs=[
            pl.BlockSpec((tm, dm), lambda i, j: (i, 0)),
            pl.BlockSpec((1, dm), lambda i, j: (0, 0)),
            pl.BlockSpec((1, dm), lambda i, j: (0, 0)),
            pl.BlockSpec((PEER_HEADS, eb, tm), lambda i, j: (0, j, i)),
            pl.BlockSpec((PEER_HEADS, eb, tm), lambda i, j: (0, j, i)),
            pl.BlockSpec((PEER_HEADS, nk, tm), lambda i, j: (0, 0, i)),
            pl.BlockSpec((PEER_HEADS, nk, tm), lambda i, j: (0, 0, i)),
            pl.BlockSpec((eb * nk, dm), lambda i, j: (j, 0)),
            pl.BlockSpec((eb * nk, dm), lambda i, j: (j, 0)),
        ],
        out_specs=pl.BlockSpec((tm, dm), lambda i, j: (i, 0)),
        out_shape=jax.ShapeDtypeStruct((t, dm), F32),
        scratch_shapes=[
            pltpu.VMEM((tm, dm), BF16),
            pltpu.VMEM((tm, eb * nk), BF16),
            pltpu.VMEM((tm, dm), F32),
        ],
        compiler_params=_cparams("parallel", "arbitrary"),
        name="peer_dense",
    )(x, g_ffn.reshape(1, dm), g_final.reshape(1, dm), e1, nrow, e2, r2, u16, v16)


def _arrange_w_in(w_in):
    d = w_in.shape[0]
    sizes = (CONV_DIM, CONV_DIM, 3 * DN_DIM, DN_DIM, DN_HEADS, DN_HEADS, XA_DIM, 2048, 2048, 2048)
    offs = [0]
    for s in sizes:
        offs.append(offs[-1] + s)
    glu = w_in[:, offs[0]:offs[2]]
    qkv = w_in[:, offs[2]:offs[3]]
    z = w_in[:, offs[3]:offs[4]]
    ab = w_in[:, offs[4]:offs[6]]
    xq = w_in[:, offs[6]:offs[7]]
    gates = w_in[:, offs[7]:offs[10]]
    pad = jnp.zeros((d, PROJ_COLS - COL_AB - 2 * DN_HEADS), w_in.dtype)
    return jnp.concatenate([qkv, glu, z, gates, xq, ab, pad], axis=1).astype(BF16)


def _layer(x, mem_k, mem_v, conv_buf, qkv_buf, delta_state, w, *, chunk, tiles, g_final):
    bsz, seq, dm = x.shape
    t = bsz * seq
    assert seq % chunk == 0
    tiles = {name: min(size, t) for name, size in tiles.items()}
    x2d = x.reshape(t, dm)
    proj = _norm_matmul(x2d, w["norm_mix"], w["w_in"], tm=tiles["tm_proj"], tn=PROJ_TN, name="in_proj")
    c, new_conv = _conv_branch(proj, conv_buf, w["conv_dw"], w["conv_dw_b"], w["conv_ln_g"], w["conv_ln_b"],
                               bsz=bsz, seq=seq, tt=tiles["tt_conv"])
    qkvc, g, beta, new_qkv = _dn_prep(proj, qkv_buf, w["dn_conv_w"], w["dn_a_log"], w["dn_dt_bias"],
                                      bsz=bsz, seq=seq, tt=tiles["tt_dn"])
    o, new_state = _dn_chunk(qkvc, proj, g, beta, delta_state, w["dn_norm_g"], bsz=bsz, seq=seq, c=chunk)
    a = _mem_attention(proj, mem_k, mem_v, bsz=bsz, seq=seq, tl=tiles["tl_attn"])
    merged = _merge(c, o, a, proj, w["w_conv_out"], w["w_dn_out"], w["w_xa_out"], tm=tiles["tm_merge"], tn=512)
    x2 = _out_proj(merged, w["w_out"], x2d, tm=tiles["tm_merge"], tn=1024)
    q = _norm_matmul(x2, w["norm_ffn"], w["w_peer_q"], tm=tiles["tm_merge"], tn=1024, name="peer_q")
    e1, e2, r2, nrow = _peer_route(q, w["peer_keys_1"], w["peer_keys_2"], tm=LANES)
    y = _peer_dense(x2, w["norm_ffn"], g_final, e1, e2, r2, nrow, w["peer_u"], w["peer_v"],
                    tm=tiles["tm_peer"], eb=8)
    return y.reshape(bsz, seq, dm), new_conv, new_qkv, new_state


PROMPT_TILES = dict(tm_proj=1024, tt_conv=256, tt_dn=128, tl_attn=256, tm_merge=512, tm_peer=256)
SAMPLE_TILES = dict(tm_proj=1024, tt_conv=8, tt_dn=8, tl_attn=8, tm_merge=512, tm_peer=256)


def kernel(x_prompt, x_sample, mem_prompt, cache_mem_k, cache_mem_v, state_conv, state_qkv_conv, state_delta, norm_mix, norm_mem, w_in, conv_dw, conv_dw_b, conv_ln_g, conv_ln_b, w_conv_out, dn_conv_w, dn_a_log, dn_dt_bias, dn_norm_g, w_dn_out, w_mem_kv, w_xa_out, w_out, norm_ffn, w_peer_q, peer_keys_1, peer_keys_2, peer_u, peer_v, norm_final):
    depth = w_in.shape[0]
    assert depth == 1, "the final norm is fused into the last layer's PEER kernel; one layer is supported"
    bp, sp, dm = x_prompt.shape
    bs, ss, _ = x_sample.shape
    n_mem = mem_prompt.shape[1]
    i = 0
    w = dict(
        norm_mix=norm_mix[i], w_in=_arrange_w_in(w_in[i]), conv_dw=conv_dw[i], conv_dw_b=conv_dw_b[i],
        conv_ln_g=conv_ln_g[i], conv_ln_b=conv_ln_b[i], w_conv_out=w_conv_out[i].astype(BF16),
        dn_conv_w=dn_conv_w[i], dn_a_log=dn_a_log[i], dn_dt_bias=dn_dt_bias[i], dn_norm_g=dn_norm_g[i],
        w_dn_out=w_dn_out[i].astype(BF16), w_xa_out=w_xa_out[i].astype(BF16), w_out=w_out[i].astype(BF16),
        norm_ffn=norm_ffn[i], w_peer_q=w_peer_q[i].astype(BF16), peer_keys_1=peer_keys_1[i],
        peer_keys_2=peer_keys_2[i], peer_u=peer_u[i].astype(BF16), peer_v=peer_v[i].astype(BF16),
    )
    mem2d = mem_prompt.reshape(bp * n_mem, dm)
    wkv = w_mem_kv[i].astype(BF16)
    mk = _norm_matmul(mem2d, norm_mem[i], wkv[:, :XA_DIM], tm=bp * n_mem, tn=512, name="mem_k")
    mv = _norm_matmul(mem2d, norm_mem[i], wkv[:, XA_DIM:], tm=bp * n_mem, tn=512, name="mem_v")
    mk = mk.reshape(bp, n_mem, XA_DIM)
    mv = mv.reshape(bp, n_mem, XA_DIM)
    yp, conv_p, qkv_p, delta_p = _layer(
        x_prompt, mk, mv,
        jnp.zeros((bp, CONV_K - 1, CONV_DIM), F32),
        jnp.zeros((bp, DN_CONV_K - 1, 3 * DN_DIM), F32),
        jnp.zeros((bp, DN_HEADS, DN_HEAD_DIM, DN_HEAD_DIM), F32),
        w, chunk=min(DN_CHUNK, sp), tiles=PROMPT_TILES, g_final=norm_final)
    ys, conv_s, qkv_s, delta_s = _layer(
        x_sample, cache_mem_k[i].reshape(bs, n_mem, XA_DIM), cache_mem_v[i].reshape(bs, n_mem, XA_DIM),
        state_conv[i], state_qkv_conv[i], state_delta[i],
        w, chunk=min(DN_CHUNK, ss), tiles=SAMPLE_TILES, g_final=norm_final)
    kv_shape = (1, bp, n_mem, XA_HEADS, XA_HEAD_DIM)
    return (yp, ys, mk.reshape(kv_shape), mv.reshape(kv_shape), conv_p[None], qkv_p[None], delta_p[None],
            conv_s[None], qkv_s[None], delta_s[None])
```

```python
import functools

import jax
import jax.numpy as jnp
from jax import lax
from jax.experimental import pallas as pl
from jax.experimental.pallas import tpu as pltpu

F32 = jnp.float32
BF16 = jnp.bfloat16
HIGHEST = lax.Precision.HIGHEST

NORM_EPS = 1e-6
LANES = 128
SUBLANES = 8
VMEM_LIMIT_BYTES = 56 * 1024 * 1024

CONV_DIM = 1024
CONV_K = 31
DN_HEADS = 16
DN_HEAD_DIM = 128
DN_DIM = DN_HEADS * DN_HEAD_DIM
DN_CONV_K = 4
DN_CHUNK = 64
XA_HEADS = 4
XA_HEAD_DIM = 256
XA_DIM = XA_HEADS * XA_HEAD_DIM
PEER_HEADS = 8
PEER_N_KEYS = 128
PEER_DK = 256
PEER_TOPK = 16

COL_QKV = 0
COL_GLU = 3 * DN_DIM
COL_Z = COL_GLU + 2 * CONV_DIM
COL_GATES = COL_Z + DN_DIM
COL_XQ = COL_GATES + 3 * 2048
COL_AB = COL_XQ + XA_DIM
PROJ_TN = 1280
PROJ_COLS = 14 * PROJ_TN


def _cparams(*dims):
    return pltpu.CompilerParams(dimension_semantics=dims, vmem_limit_bytes=VMEM_LIMIT_BYTES)


def _dot(a, b, precision=None):
    return jnp.dot(a, b, preferred_element_type=F32, precision=precision)


def _dot_nt(a, b):
    return lax.dot_general(a, b, (((1,), (1,)), ((), ())), preferred_element_type=F32)


def _dot_tn(a, b):
    return lax.dot_general(a, b, (((0,), (0,)), ((), ())), preferred_element_type=F32)


def _rms(x, g):
    return x * lax.rsqrt(jnp.mean(x * x, axis=-1, keepdims=True) + NORM_EPS) * g


def _silu(x):
    return x * jax.nn.sigmoid(x)


def _norm_matmul_kernel(x_ref, g_ref, w_ref, o_ref, h_ref):
    @pl.when(pl.program_id(1) == 0)
    def _():
        h_ref[...] = _rms(x_ref[...], g_ref[...]).astype(BF16)

    o_ref[...] = _dot(h_ref[...], w_ref[...])


def _norm_matmul(x, g, w, *, tm, tn, name):
    t, d = x.shape
    n = w.shape[1]
    return pl.pallas_call(
        _norm_matmul_kernel,
        grid=(t // tm, n // tn),
        in_specs=[
            pl.BlockSpec((tm, d), lambda i, j: (i, 0)),
            pl.BlockSpec((1, d), lambda i, j: (0, 0)),
            pl.BlockSpec((d, tn), lambda i, j: (0, j)),
        ],
        out_specs=pl.BlockSpec((tm, tn), lambda i, j: (i, j)),
        out_shape=jax.ShapeDtypeStruct((t, n), F32),
        scratch_shapes=[pltpu.VMEM((tm, d), BF16)],
        compiler_params=_cparams("parallel", "arbitrary"),
        name=name,
    )(x, g.reshape(1, d), w)


def _conv_kernel(a_ref, g_ref, buf_ref, w_ref, b_ref, lng_ref, lnb_ref, c_ref, nbuf_ref, xp_ref, y_ref, *, tt, nt):
    i = pl.program_id(1)
    head = 32
    off = head - (CONV_K - 1)

    @pl.when(i == 0)
    def _():
        xp_ref[0:off, :] = jnp.zeros((off, CONV_DIM), F32)
        xp_ref[off:head, :] = buf_ref[0]

    xp_ref[head:head + tt, :] = a_ref[...] * jax.nn.sigmoid(g_ref[...])

    rt = min(32, tt)
    cw = 512
    for r in range(tt // rt):
        for cb in range(CONV_DIM // cw):
            cs = slice(cb * cw, (cb + 1) * cw)
            acc = jnp.zeros((rt, cw), F32)
            for j in range(CONV_K):
                acc = acc + w_ref[j:j + 1, cs] * xp_ref[pl.ds(r * rt + off + j, rt), cs]
            y_ref[r * rt:(r + 1) * rt, cs] = acc + b_ref[:, cs]

    y = y_ref[...]
    mu = jnp.mean(y, axis=-1, keepdims=True)
    yc = y - mu
    var = jnp.mean(yc * yc, axis=-1, keepdims=True)
    c_ref[...] = _silu(yc * lax.rsqrt(var + NORM_EPS) * lng_ref[...] + lnb_ref[...])

    tail = xp_ref[tt:tt + head, :]

    @pl.when(i == nt - 1)
    def _():
        nbuf_ref[0] = tail[off:, :]

    xp_ref[0:head, :] = tail


def _conv_branch(proj, buf, w, b, lng, lnb, *, bsz, seq, tt):
    nt = seq // tt
    ca = COL_GLU // CONV_DIM
    row = lambda b_, i: b_ * nt + i
    vec = lambda v: v.reshape(1, CONV_DIM)
    return pl.pallas_call(
        functools.partial(_conv_kernel, tt=tt, nt=nt),
        grid=(bsz, nt),
        in_specs=[
            pl.BlockSpec((tt, CONV_DIM), lambda b_, i: (row(b_, i), ca)),
            pl.BlockSpec((tt, CONV_DIM), lambda b_, i: (row(b_, i), ca + 1)),
            pl.BlockSpec((1, CONV_K - 1, CONV_DIM), lambda b_, i: (b_, 0, 0)),
            pl.BlockSpec((CONV_K, CONV_DIM), lambda b_, i: (0, 0)),
            pl.BlockSpec((1, CONV_DIM), lambda b_, i: (0, 0)),
            pl.BlockSpec((1, CONV_DIM), lambda b_, i: (0, 0)),
            pl.BlockSpec((1, CONV_DIM), lambda b_, i: (0, 0)),
        ],
        out_specs=[
            pl.BlockSpec((tt, CONV_DIM), lambda b_, i: (row(b_, i), 0)),
            pl.BlockSpec((1, CONV_K - 1, CONV_DIM), lambda b_, i: (b_, 0, 0)),
        ],
        out_shape=[
            jax.ShapeDtypeStruct((bsz * seq, CONV_DIM), F32),
            jax.ShapeDtypeStruct((bsz, CONV_K - 1, CONV_DIM), F32),
        ],
        scratch_shapes=[pltpu.VMEM((tt + 32, CONV_DIM), F32), pltpu.VMEM((tt, CONV_DIM), F32)],
        compiler_params=_cparams("parallel", "arbitrary"),
        name="conv_branch",
    )(proj, proj, buf, w, vec(b), vec(lng), vec(lnb))


def _dn_prep_kernel(x_ref, ab_ref, buf_ref, w_ref, alog_ref, dtb_ref, o_ref, g_ref, beta_ref, nbuf_ref, xp_ref, *, tt, nt):
    i = pl.program_id(1)
    head = SUBLANES
    off = head - (DN_CONV_K - 1)

    @pl.when(i == 0)
    def _():
        xp_ref[0:off, :] = jnp.zeros((off, 3 * DN_DIM), F32)
        xp_ref[off:head, :] = buf_ref[0]

    xp_ref[head:head + tt, :] = x_ref[...]

    for blk in range(3 * DN_HEADS):
        cs = slice(blk * DN_HEAD_DIM, (blk + 1) * DN_HEAD_DIM)
        acc = jnp.zeros((tt, DN_HEAD_DIM), F32)
        for j in range(DN_CONV_K):
            acc = acc + w_ref[j:j + 1, cs] * xp_ref[pl.ds(off + j, tt), cs]
        y = _silu(acc)
        if blk < 2 * DN_HEADS:
            y = y * lax.rsqrt(jnp.sum(y * y, axis=-1, keepdims=True) + NORM_EPS)
            if blk < DN_HEADS:
                y = y * (DN_HEAD_DIM ** -0.5)
        o_ref[:, cs] = y

    ab = ab_ref[...]
    a_raw = ab[:, 0:DN_HEADS]
    b_raw = ab[:, DN_HEADS:2 * DN_HEADS]
    sp_in = a_raw + dtb_ref[...]
    softplus = jnp.maximum(sp_in, 0.0) + jnp.log1p(jnp.exp(-jnp.abs(sp_in)))
    g_ref[...] = -jnp.exp(alog_ref[...]) * softplus
    beta_ref[...] = jax.nn.sigmoid(b_raw)

    tail = xp_ref[tt:tt + head, :]

    @pl.when(i == nt - 1)
    def _():
        nbuf_ref[0] = tail[off:, :]

    xp_ref[0:head, :] = tail


def _dn_prep(proj, buf, w, a_log, dt_bias, *, bsz, seq, tt):
    nt = seq // tt
    dq = 3 * DN_DIM
    row = lambda b_, i: b_ * nt + i
    return pl.pallas_call(
        functools.partial(_dn_prep_kernel, tt=tt, nt=nt),
        grid=(bsz, nt),
        in_specs=[
            pl.BlockSpec((tt, dq), lambda b_, i: (row(b_, i), COL_QKV // dq)),
            pl.BlockSpec((tt, LANES), lambda b_, i: (row(b_, i), COL_AB // LANES)),
            pl.BlockSpec((1, DN_CONV_K - 1, dq), lambda b_, i: (b_, 0, 0)),
            pl.BlockSpec((DN_CONV_K, dq), lambda b_, i: (0, 0)),
            pl.BlockSpec((1, DN_HEADS), lambda b_, i: (0, 0)),
            pl.BlockSpec((1, DN_HEADS), lambda b_, i: (0, 0)),
        ],
        out_specs=[
            pl.BlockSpec((tt, dq), lambda b_, i: (row(b_, i), 0)),
            pl.BlockSpec((tt, DN_HEADS), lambda b_, i: (row(b_, i), 0)),
            pl.BlockSpec((tt, DN_HEADS), lambda b_, i: (row(b_, i), 0)),
            pl.BlockSpec((1, DN_CONV_K - 1, dq), lambda b_, i: (b_, 0, 0)),
        ],
        out_shape=[
            jax.ShapeDtypeStruct((bsz * seq, dq), F32),
            jax.ShapeDtypeStruct((bsz * seq, DN_HEADS), F32),
            jax.ShapeDtypeStruct((bsz * seq, DN_HEADS), F32),
            jax.ShapeDtypeStruct((bsz, DN_CONV_K - 1, dq), F32),
        ],
        scratch_shapes=[pltpu.VMEM((tt + SUBLANES, dq), F32)],
        compiler_params=_cparams("parallel", "arbitrary"),
        name="dn_prep",
    )(proj, proj, buf, w, a_log.reshape(1, DN_HEADS), dt_bias.reshape(1, DN_HEADS))


def _dot_split(a, b):
    a_hi = a.astype(BF16)
    b_hi = b.astype(BF16)
    a_lo = (a - a_hi.astype(F32)).astype(BF16)
    b_lo = (b - b_hi.astype(F32)).astype(BF16)
    return _dot(a_hi, b_hi) + (_dot(a_hi, b_lo) + _dot(a_lo, b_hi))


def _run_interleaved(gens):
    while gens:
        alive = []
        for g in gens:
            try:
                next(g)
                alive.append(g)
            except StopIteration:
                pass
        gens = alive


def _unit_lower_inverse(a, masks):
    eye, diag_block, off_blocks = masks
    ad = jnp.where(diag_block, a, 0.0)
    a2 = _dot_split(ad, ad)
    yield
    a4 = _dot_split(a2, a2)
    x = _dot_split(eye - ad, eye + a2)
    yield
    x = _dot_split(x, eye + a4)
    yield
    for off_block in off_blocks:
        y = _dot_split(x, jnp.where(off_block, a, 0.0))
        yield
        x = x - _dot_split(y, x)
        yield
    return x


def _inverse_masks(c):
    ri = lax.broadcasted_iota(jnp.int32, (c, c), 0)
    ci = lax.broadcasted_iota(jnp.int32, (c, c), 1)
    eye = (ri == ci).astype(F32)
    diag_block = (ri >> 3) == (ci >> 3)
    off_blocks = []
    shift = 3
    while (1 << shift) < c:
        off_blocks.append(((ri >> (shift + 1)) == (ci >> (shift + 1))) & ((ri >> shift) != (ci >> shift)))
        shift += 1
    return eye, diag_block, off_blocks


def _dn_head(q, k, v, z, gcol, beta, grow, s, ng, tri, inv_masks, c):
    incl, strict, upper = tri
    gc_col = jnp.sum(jnp.where(incl, grow, 0.0), axis=1, keepdims=True)
    gc_row = jnp.sum(jnp.where(upper, gcol, 0.0), axis=0, keepdims=True)
    decay_incl = jnp.exp(jnp.where(incl, gc_col - gc_row, -jnp.inf))
    decay_strict = jnp.where(strict, decay_incl, 0.0)

    kb = k * beta
    kq = _dot_nt(jnp.concatenate([kb, q], axis=0).astype(BF16), k.astype(BF16))
    yield
    t_inv = yield from _unit_lower_inverse(kq[:c] * decay_strict, inv_masks)
    e_gc = jnp.exp(gc_col)
    sol = _dot_split(t_inv, jnp.concatenate([v * beta, kb * e_gc], axis=1))
    yield
    value = sol[:, :DN_HEAD_DIM]
    k_cum = sol[:, DN_HEAD_DIM:]
    qk = kq[c:] * decay_incl
    gc_last = gc_col[c - 1:c, :]
    k_dec = k * jnp.exp(gc_last - gc_col)

    ks = _dot(jnp.concatenate([k_cum, q * e_gc], axis=0).astype(BF16), s.astype(BF16))
    yield
    u16 = (value - ks[:c]).astype(BF16)
    o = ks[c:] + _dot(qk.astype(BF16), u16)
    s_new = s * jnp.exp(gc_last) + _dot_tn(k_dec.astype(BF16), u16)
    return _rms(o, ng) * _silu(z), s_new


def _dn_chunk_kernel(q_ref, k_ref, v_ref, z_ref, g_ref, beta_ref, grow_ref, s0_ref, ng_ref, o_ref, s_ref, *, c):
    @pl.when(pl.program_id(1) == 0)
    def _():
        s_ref[...] = s0_ref[...]

    ri = lax.broadcasted_iota(jnp.int32, (c, c), 0)
    ci = lax.broadcasted_iota(jnp.int32, (c, c), 1)
    tri = (ri >= ci, ri > ci, ri <= ci)
    inv_masks = _inverse_masks(c)
    ng = ng_ref[...]
    d = DN_HEAD_DIM

    def head(h):
        cs = slice(h * d, (h + 1) * d)
        o, s_new = yield from _dn_head(q_ref[:, cs], k_ref[:, cs], v_ref[:, cs], z_ref[:, cs], g_ref[:, h:h + 1],
                                       beta_ref[:, h:h + 1], grow_ref[0, 0, h:h + 1, :], s_ref[0, h], ng, tri,
                                       inv_masks, c)
        o_ref[:, cs] = o
        s_ref[0, h] = s_new

    _run_interleaved([head(h) for h in range(DN_HEADS)])


def _dn_chunk(qkvc, proj, g, beta, s0, norm_g, *, bsz, seq, c):
    n = seq // c
    h_ = DN_HEADS
    d = DN_HEAD_DIM
    grow = jnp.transpose(g.reshape(bsz, n, c, h_), (0, 1, 3, 2))
    row = lambda b_, n_: (b_ * n + n_)
    return pl.pallas_call(
        functools.partial(_dn_chunk_kernel, c=c),
        grid=(bsz, n),
        in_specs=[
            pl.BlockSpec((c, DN_DIM), lambda b_, n_: (row(b_, n_), 0)),
            pl.BlockSpec((c, DN_DIM), lambda b_, n_: (row(b_, n_), 1)),
            pl.BlockSpec((c, DN_DIM), lambda b_, n_: (row(b_, n_), 2)),
            pl.BlockSpec((c, DN_DIM), lambda b_, n_: (row(b_, n_), COL_Z // DN_DIM)),
            pl.BlockSpec((c, h_), lambda b_, n_: (row(b_, n_), 0)),
            pl.BlockSpec((c, h_), lambda b_, n_: (row(b_, n_), 0)),
            pl.BlockSpec((1, 1, h_, c), lambda b_, n_: (b_, n_, 0, 0)),
            pl.BlockSpec((1, h_, d, d), lambda b_, n_: (b_, 0, 0, 0)),
            pl.BlockSpec((1, d), lambda b_, n_: (0, 0)),
        ],
        out_specs=[
            pl.BlockSpec((c, DN_DIM), lambda b_, n_: (row(b_, n_), 0)),
            pl.BlockSpec((1, h_, d, d), lambda b_, n_: (b_, 0, 0, 0)),
        ],
        out_shape=[
            jax.ShapeDtypeStruct((bsz * seq, DN_DIM), F32),
            jax.ShapeDtypeStruct((bsz, h_, d, d), F32),
        ],
        compiler_params=_cparams("parallel", "arbitrary"),
        name="dn_chunk",
    )(qkvc, qkvc, qkvc, proj, g, beta, grow, s0, norm_g.reshape(1, d))


def _mem_attn_kernel(q_ref, k_ref, v_ref, o_ref):
    for h in range(XA_HEADS):
        cs = slice(h * XA_HEAD_DIM, (h + 1) * XA_HEAD_DIM)
        s = _dot_nt(q_ref[:, cs].astype(BF16), k_ref[0, :, cs].astype(BF16)) * (XA_HEAD_DIM ** -0.5)
        e = jnp.exp(s - jnp.max(s, axis=-1, keepdims=True))
        p = e / jnp.sum(e, axis=-1, keepdims=True)
        o_ref[:, cs] = _dot(p.astype(BF16), v_ref[0, :, cs].astype(BF16))


def _mem_attention(proj, mem_k, mem_v, *, bsz, seq, tl):
    nl = seq // tl
    m = mem_k.shape[1]
    return pl.pallas_call(
        _mem_attn_kernel,
        grid=(bsz, nl),
        in_specs=[
            pl.BlockSpec((tl, XA_DIM), lambda b_, i: (b_ * nl + i, COL_XQ // XA_DIM)),
            pl.BlockSpec((1, m, XA_DIM), lambda b_, i: (b_, 0, 0)),
            pl.BlockSpec((1, m, XA_DIM), lambda b_, i: (b_, 0, 0)),
        ],
        out_specs=pl.BlockSpec((tl, XA_DIM), lambda b_, i: (b_ * nl + i, 0)),
        out_shape=jax.ShapeDtypeStruct((bsz * seq, XA_DIM), F32),
        compiler_params=_cparams("parallel", "arbitrary"),
        name="mem_attention",
    )(proj, mem_k, mem_v)


def _merge_kernel(c_ref, o_ref, a_ref, gc_ref, gd_ref, gx_ref, wc_ref, wd_ref, wx_ref, m_ref):
    yc = _dot(c_ref[...].astype(BF16), wc_ref[...])
    yd = _dot(o_ref[...].astype(BF16), wd_ref[...])
    yx = _dot(a_ref[...].astype(BF16), wx_ref[...])
    m_ref[...] = (jax.nn.sigmoid(gc_ref[...]) * yc + jax.nn.sigmoid(gd_ref[...]) * yd
                  + jax.nn.sigmoid(gx_ref[...]) * yx).astype(BF16)


def _merge(c, o, a, proj, wc, wd, wx, *, tm, tn):
    t = c.shape[0]
    dm = wc.shape[1]
    gcol = COL_GATES // tn
    gstep = dm // tn
    return pl.pallas_call(
        _merge_kernel,
        grid=(t // tm, dm // tn),
        in_specs=[
            pl.BlockSpec((tm, CONV_DIM), lambda i, j: (i, 0)),
            pl.BlockSpec((tm, DN_DIM), lambda i, j: (i, 0)),
            pl.BlockSpec((tm, XA_DIM), lambda i, j: (i, 0)),
            pl.BlockSpec((tm, tn), lambda i, j: (i, gcol + j)),
            pl.BlockSpec((tm, tn), lambda i, j: (i, gcol + gstep + j)),
            pl.BlockSpec((tm, tn), lambda i, j: (i, gcol + 2 * gstep + j)),
            pl.BlockSpec((CONV_DIM, tn), lambda i, j: (0, j)),
            pl.BlockSpec((DN_DIM, tn), lambda i, j: (0, j)),
            pl.BlockSpec((XA_DIM, tn), lambda i, j: (0, j)),
        ],
        out_specs=pl.BlockSpec((tm, tn), lambda i, j: (i, j)),
        out_shape=jax.ShapeDtypeStruct((t, dm), BF16),
        compiler_params=_cparams("parallel", "arbitrary"),
        name="merge",
    )(c, o, a, proj, proj, proj, wc, wd, wx)


def _out_proj_kernel(m_ref, w_ref, x_ref, o_ref):
    o_ref[...] = x_ref[...] + _dot(m_ref[...], w_ref[...])


def _out_proj(merged, w, x, *, tm, tn):
    t, dm = x.shape
    return pl.pallas_call(
        _out_proj_kernel,
        grid=(t // tm, dm // tn),
        in_specs=[
            pl.BlockSpec((tm, dm), lambda i, j: (i, 0)),
            pl.BlockSpec((dm, tn), lambda i, j: (0, j)),
            pl.BlockSpec((tm, tn), lambda i, j: (i, j)),
        ],
        out_specs=pl.BlockSpec((tm, tn), lambda i, j: (i, j)),
        out_shape=jax.ShapeDtypeStruct((t, dm), F32),
        compiler_params=_cparams("parallel", "arbitrary"),
        name="out_proj",
    )(merged, w, x)


def _top16(s, tm):
    key = lax.broadcasted_iota(jnp.int32, s.shape, 0).astype(F32)
    work = s
    rank = jnp.full(s.shape, float(PEER_TOPK), F32)
    vals = []
    for a in range(PEER_TOPK):
        m = jnp.max(work, axis=0, keepdims=True)
        idx = jnp.min(jnp.where(work == m, key, float(PEER_N_KEYS)), axis=0, keepdims=True)
        sel = key == idx
        rank = jnp.where(sel, float(a), rank)
        work = jnp.where(sel, -jnp.inf, work)
        vals.append(m)
    return vals, rank


def _peer_route_kernel(q_ref, k1_ref, k2_ref, e1_ref, e2_ref, r2_ref, nrow_ref, *, tm):
    half = PEER_DK // 2
    q = q_ref[...]
    s1 = _dot_nt(k1_ref[...].astype(BF16), q[:, :half].astype(BF16))
    s2 = _dot_nt(k2_ref[...].astype(BF16), q[:, half:].astype(BF16))
    t1, rank1 = _top16(s1, tm)
    t2, rank2 = _top16(s2, tm)

    k = PEER_TOPK
    t2s = jnp.concatenate(t2, axis=0)
    sub = lax.broadcasted_iota(jnp.int32, (k, tm), 0).astype(F32)
    work = [t1[a] + t2s for a in range(k)]
    pos = [sub + float(a * k) for a in range(k)]
    tops = []
    for _ in range(k):
        m = work[0]
        for a in range(1, k):
            m = jnp.maximum(m, work[a])
        m = jnp.max(m, axis=0, keepdims=True)
        p = jnp.where(work[0] == m, pos[0], float(k * k))
        for a in range(1, k):
            p = jnp.minimum(p, jnp.where(work[a] == m, pos[a], float(k * k)))
        p = jnp.min(p, axis=0, keepdims=True)
        work = [jnp.where(pos[a] == p, -jnp.inf, work[a]) for a in range(k)]
        tops.append(m)
    zsum = jnp.zeros((1, tm), F32)
    for m in tops:
        zsum = zsum + jnp.exp(m - tops[0])
    nrow = jnp.zeros(s1.shape, F32)
    for a in range(k):
        n_a = jnp.sum((work[a] == -jnp.inf).astype(F32), axis=0, keepdims=True)
        nrow = jnp.where(rank1 == float(a), n_a, nrow)

    e1_ref[0] = jnp.exp(s1 - t1[0]) / zsum
    e2_ref[0] = jnp.exp(s2 - t2[0])
    r2_ref[0] = rank2
    nrow_ref[0] = nrow


def _peer_route(q, k1, k2, *, tm):
    t = q.shape[0]
    nk = PEER_N_KEYS
    out = jax.ShapeDtypeStruct((PEER_HEADS, nk, t), F32)
    spec = pl.BlockSpec((1, nk, tm), lambda i, h: (h, 0, i))
    return pl.pallas_call(
        functools.partial(_peer_route_kernel, tm=tm),
        grid=(t // tm, PEER_HEADS),
        in_specs=[
            pl.BlockSpec((tm, PEER_DK), lambda i, h: (i, h)),
            pl.BlockSpec((nk, PEER_DK // 2), lambda i, h: (0, 0)),
            pl.BlockSpec((nk, PEER_DK // 2), lambda i, h: (0, 0)),
        ],
        out_specs=[spec, spec, spec, spec],
        out_shape=[out, out, out, out],
        compiler_params=_cparams("parallel", "arbitrary"),
        name="peer_route",
    )(q, k1, k2)


def _gelu_tanh(x):
    return 0.5 * x * (1.0 + jnp.tanh(0.7978845608028654 * (x + 0.044715 * (x * x * x))))


def _peer_dense_kernel(x_ref, gn_ref, gf_ref, e1_ref, nrow_ref, e2_ref, r2_ref, u_ref, v_ref, y_ref,
                       h_ref, s_ref, p_ref, *, tm, eb, nj):
    j = pl.program_id(1)
    nk = PEER_N_KEYS

    @pl.when(j == 0)
    def _():
        h_ref[...] = _rms(x_ref[...], gn_ref[...]).astype(BF16)
        y_ref[...] = jnp.zeros_like(y_ref)

    s_ref[...] = _dot_nt(u_ref[...], h_ref[...])
    for bi in range(eb):
        for th in range(tm // LANES):
            ts = slice(th * LANES, (th + 1) * LANES)
            w = jnp.zeros((nk, LANES), F32)
            for h in range(PEER_HEADS):
                sel = r2_ref[h, :, ts] < nrow_ref[h, bi:bi + 1, ts]
                w = w + jnp.where(sel, e1_ref[h, bi:bi + 1, ts] * e2_ref[h, :, ts], 0.0)
            act = _gelu_tanh(s_ref[bi * nk:(bi + 1) * nk, ts])
            p_ref[ts, bi * nk:(bi + 1) * nk] = (w * act).T.astype(BF16)
    y_ref[...] += _dot(p_ref[...], v_ref[...])

    @pl.when(j == nj - 1)
    def _():
        y_ref[...] = _rms(x_ref[...] + y_ref[...], gf_ref[...])


def _peer_dense(x, g_ffn, g_final, e1, e2, r2, nrow, u16, v16, *, tm, eb):
    t, dm = x.shape
    nk = PEER_N_KEYS
    ne = u16.shape[0]
    nj = ne // (eb * nk)
    return pl.pallas_call(
        functools.partial(_peer_dense_kernel, tm=tm, eb=eb, nj=nj),
        grid=(t // tm, nj),
        in_specs=[
            pl.BlockSpec((tm, dm), lambda i, j: (i, 0)),
            pl.BlockSpec((1, dm), lambda i, j: (0, 0)),
            pl.BlockSpec((1, dm), lambda i, j: (0, 0)),
            pl.BlockSpec((PEER_HEADS, eb, tm), lambda i, j: (0, j, i)),
            pl.BlockSpec((PEER_HEADS, eb, tm), lambda i, j: (0, j, i)),
            pl.BlockSpec((PEER_HEADS, nk, tm), lambda i, j: (0, 0, i)),
            pl.BlockSpec((PEER_HEADS, nk, tm), lambda i, j: (0, 0, i)),
            pl.BlockSpec((eb * nk, dm), lambda i, j: (j, 0)),
            pl.BlockSpec((eb * nk, dm), lambda i, j: (j, 0)),
        ],
        out_specs=pl.BlockSpec((tm, dm), lambda i, j: (i, 0)),
        out_shape=jax.ShapeDtypeStruct((t, dm), F32),
        scratch_shapes=[
            pltpu.VMEM((tm, dm), BF16),
            pltpu.VMEM((eb * nk, tm), F32),
            pltpu.VMEM((tm, eb * nk), BF16),
        ],
        compiler_params=_cparams("parallel", "arbitrary"),
        name="peer_dense",
    )(x, g_ffn.reshape(1, dm), g_final.reshape(1, dm), e1, nrow, e2, r2, u16, v16)


def _arrange_w_in(w_in):
    d = w_in.shape[0]
    sizes = (CONV_DIM, CONV_DIM, 3 * DN_DIM, DN_DIM, DN_HEADS, DN_HEADS, XA_DIM, 2048, 2048, 2048)
    offs = [0]
    for s in sizes:
        offs.append(offs[-1] + s)
    glu = w_in[:, offs[0]:offs[2]]
    qkv = w_in[:, offs[2]:offs[3]]
    z = w_in[:, offs[3]:offs[4]]
    ab = w_in[:, offs[4]:offs[6]]
    xq = w_in[:, offs[6]:offs[7]]
    gates = w_in[:, offs[7]:offs[10]]
    pad = jnp.zeros((d, PROJ_COLS - COL_AB - 2 * DN_HEADS), w_in.dtype)
    return jnp.concatenate([qkv, glu, z, gates, xq, ab, pad], axis=1).astype(BF16)


def _layer(x, mem_k, mem_v, conv_buf, qkv_buf, delta_state, w, *, chunk, tiles, g_final):
    bsz, seq, dm = x.shape
    t = bsz * seq
    assert seq % chunk == 0
    tiles = {name: min(size, t) for name, size in tiles.items()}
    x2d = x.reshape(t, dm)
    proj = _norm_matmul(x2d, w["norm_mix"], w["w_in"], tm=tiles["tm_proj"], tn=PROJ_TN, name="in_proj")
    c, new_conv = _conv_branch(proj, conv_buf, w["conv_dw"], w["conv_dw_b"], w["conv_ln_g"], w["conv_ln_b"],
                               bsz=bsz, seq=seq, tt=tiles["tt_conv"])
    qkvc, g, beta, new_qkv = _dn_prep(proj, qkv_buf, w["dn_conv_w"], w["dn_a_log"], w["dn_dt_bias"],
                                      bsz=bsz, seq=seq, tt=tiles["tt_dn"])
    o, new_state = _dn_chunk(qkvc, proj, g, beta, delta_state, w["dn_norm_g"], bsz=bsz, seq=seq, c=chunk)
    a = _mem_attention(proj, mem_k, mem_v, bsz=bsz, seq=seq, tl=tiles["tl_attn"])
    merged = _merge(c, o, a, proj, w["w_conv_out"], w["w_dn_out"], w["w_xa_out"], tm=tiles["tm_merge"], tn=512)
    x2 = _out_proj(merged, w["w_out"], x2d, tm=tiles["tm_merge"], tn=1024)
    q = _norm_matmul(x2, w["norm_ffn"], w["w_peer_q"], tm=tiles["tm_merge"], tn=1024, name="peer_q")
    e1, e2, r2, nrow = _peer_route(q, w["peer_keys_1"], w["peer_keys_2"], tm=LANES)
    y = _peer_dense(x2, w["norm_ffn"], g_final, e1, e2, r2, nrow, w["peer_u"], w["peer_v"],
                    tm=tiles["tm_peer"], eb=8)
    return y.reshape(bsz, seq, dm), new_conv, new_qkv, new_state


PROMPT_TILES = dict(tm_proj=1024, tt_conv=256, tt_dn=128, tl_attn=256, tm_merge=512, tm_peer=512)
SAMPLE_TILES = dict(tm_proj=1024, tt_conv=8, tt_dn=8, tl_attn=8, tm_merge=512, tm_peer=512)


def kernel(x_prompt, x_sample, mem_prompt, cache_mem_k, cache_mem_v, state_conv, state_qkv_conv, state_delta, norm_mix, norm_mem, w_in, conv_dw, conv_dw_b, conv_ln_g, conv_ln_b, w_conv_out, dn_conv_w, dn_a_log, dn_dt_bias, dn_norm_g, w_dn_out, w_mem_kv, w_xa_out, w_out, norm_ffn, w_peer_q, peer_keys_1, peer_keys_2, peer_u, peer_v, norm_final):
    depth = w_in.shape[0]
    assert depth == 1, "the final norm is fused into the last layer's PEER kernel; one layer is supported"
    bp, sp, dm = x_prompt.shape
    bs, ss, _ = x_sample.shape
    n_mem = mem_prompt.shape[1]
    def layer0(a):
        return a.reshape(a.shape[1:])

    w = dict(
        norm_mix=layer0(norm_mix), w_in=_arrange_w_in(layer0(w_in)), conv_dw=layer0(conv_dw),
        conv_dw_b=layer0(conv_dw_b), conv_ln_g=layer0(conv_ln_g), conv_ln_b=layer0(conv_ln_b),
        w_conv_out=layer0(w_conv_out).astype(BF16), dn_conv_w=layer0(dn_conv_w), dn_a_log=layer0(dn_a_log),
        dn_dt_bias=layer0(dn_dt_bias), dn_norm_g=layer0(dn_norm_g), w_dn_out=layer0(w_dn_out).astype(BF16),
        w_xa_out=layer0(w_xa_out).astype(BF16), w_out=layer0(w_out).astype(BF16), norm_ffn=layer0(norm_ffn),
        w_peer_q=layer0(w_peer_q).astype(BF16), peer_keys_1=layer0(peer_keys_1), peer_keys_2=layer0(peer_keys_2),
        peer_u=layer0(peer_u).astype(BF16), peer_v=layer0(peer_v).astype(BF16),
    )
    mem2d = mem_prompt.reshape(bp * n_mem, dm)
    wkv = layer0(w_mem_kv).astype(BF16)
    mk = _norm_matmul(mem2d, layer0(norm_mem), wkv[:, :XA_DIM], tm=bp * n_mem, tn=512, name="mem_k")
    mv = _norm_matmul(mem2d, layer0(norm_mem), wkv[:, XA_DIM:], tm=bp * n_mem, tn=512, name="mem_v")
    mk = mk.reshape(bp, n_mem, XA_DIM)
    mv = mv.reshape(bp, n_mem, XA_DIM)
    yp, conv_p, qkv_p, delta_p = _layer(
        x_prompt, mk, mv,
        jnp.zeros((bp, CONV_K - 1, CONV_DIM), F32),
        jnp.zeros((bp, DN_CONV_K - 1, 3 * DN_DIM), F32),
        jnp.zeros((bp, DN_HEADS, DN_HEAD_DIM, DN_HEAD_DIM), F32),
        w, chunk=min(DN_CHUNK, sp), tiles=PROMPT_TILES, g_final=norm_final)
    ys, conv_s, qkv_s, delta_s = _layer(
        x_sample, cache_mem_k.reshape(bs, n_mem, XA_DIM), cache_mem_v.reshape(bs, n_mem, XA_DIM),
        layer0(state_conv), layer0(state_qkv_conv), layer0(state_delta),
        w, chunk=min(DN_CHUNK, ss), tiles=SAMPLE_TILES, g_final=norm_final)
    kv_shape = (1, bp, n_mem, XA_HEADS, XA_HEAD_DIM)
    return (yp, ys, mk.reshape(kv_shape), mv.reshape(kv_shape), conv_p[None], qkv_p[None], delta_p[None],
            conv_s[None], qkv_s[None], delta_s[None])
```

```python
import functools

import jax
import jax.numpy as jnp
from jax import lax
from jax.experimental import pallas as pl
from jax.experimental.pallas import tpu as pltpu

F32 = jnp.float32
BF16 = jnp.bfloat16
HIGHEST = lax.Precision.HIGHEST

NORM_EPS = 1e-6
LANES = 128
SUBLANES = 8
VMEM_LIMIT_BYTES = 56 * 1024 * 1024

CONV_DIM = 1024
CONV_K = 31
DN_HEADS = 16
DN_HEAD_DIM = 128
DN_DIM = DN_HEADS * DN_HEAD_DIM
DN_CONV_K = 4
DN_CHUNK = 64
XA_HEADS = 4
XA_HEAD_DIM = 256
XA_DIM = XA_HEADS * XA_HEAD_DIM
PEER_HEADS = 8
PEER_N_KEYS = 128
PEER_DK = 256
PEER_TOPK = 16

COL_QKV = 0
COL_GLU = 3 * DN_DIM
COL_Z = COL_GLU + 2 * CONV_DIM
COL_GATES = COL_Z + DN_DIM
COL_XQ = COL_GATES + 3 * 2048
COL_AB = COL_XQ + XA_DIM
PROJ_TN = 1280
PROJ_COLS = 14 * PROJ_TN


def _cparams(*dims):
    return pltpu.CompilerParams(dimension_semantics=dims, vmem_limit_bytes=VMEM_LIMIT_BYTES)


def _dot(a, b, precision=None):
    return jnp.dot(a, b, preferred_element_type=F32, precision=precision)


def _dot_nt(a, b):
    return lax.dot_general(a, b, (((1,), (1,)), ((), ())), preferred_element_type=F32)


def _dot_tn(a, b):
    return lax.dot_general(a, b, (((0,), (0,)), ((), ())), preferred_element_type=F32)


def _rms(x, g):
    return x * lax.rsqrt(jnp.mean(x * x, axis=-1, keepdims=True) + NORM_EPS) * g


def _silu(x):
    return x * jax.nn.sigmoid(x)


def _norm_matmul_kernel(x_ref, g_ref, w_ref, o_ref, h_ref):
    @pl.when(pl.program_id(1) == 0)
    def _():
        h_ref[...] = _rms(x_ref[...], g_ref[...]).astype(BF16)

    o_ref[...] = _dot(h_ref[...], w_ref[...])


def _norm_matmul(x, g, w, *, tm, tn, name):
    t, d = x.shape
    n = w.shape[1]
    return pl.pallas_call(
        _norm_matmul_kernel,
        grid=(t // tm, n // tn),
        in_specs=[
            pl.BlockSpec((tm, d), lambda i, j: (i, 0)),
            pl.BlockSpec((1, d), lambda i, j: (0, 0)),
            pl.BlockSpec((d, tn), lambda i, j: (0, j)),
        ],
        out_specs=pl.BlockSpec((tm, tn), lambda i, j: (i, j)),
        out_shape=jax.ShapeDtypeStruct((t, n), F32),
        scratch_shapes=[pltpu.VMEM((tm, d), BF16)],
        compiler_params=_cparams("parallel", "arbitrary"),
        name=name,
    )(x, g.reshape(1, d), w)


def _conv_kernel(a_ref, g_ref, buf_ref, w_ref, b_ref, lng_ref, lnb_ref, c_ref, nbuf_ref, xp_ref, sh_ref, y_ref,
                 *, tt, nt):
    i = pl.program_id(1)
    head = 32
    off = head - (CONV_K - 1)

    @pl.when(i == 0)
    def _():
        xp_ref[0:off, :] = jnp.zeros((off, CONV_DIM), F32)
        xp_ref[off:head, :] = buf_ref[0]

    xp_ref[head:head + tt, :] = a_ref[...] * jax.nn.sigmoid(g_ref[...])

    span = tt + head - SUBLANES
    for s in range(1, SUBLANES):
        sh_ref[s - 1] = xp_ref[pl.ds(s, span), :]

    rt = min(32, tt)
    cw = 512
    for r in range(tt // rt):
        for cb in range(CONV_DIM // cw):
            cs = slice(cb * cw, (cb + 1) * cw)
            acc = jnp.zeros((rt, cw), F32)
            for j in range(CONV_K):
                s = (off + j) % SUBLANES
                rows = pl.ds(r * rt + off + j - s, rt)
                tap = xp_ref[rows, cs] if s == 0 else sh_ref[s - 1, rows, cs]
                acc = acc + w_ref[j:j + 1, cs] * tap
            y_ref[r * rt:(r + 1) * rt, cs] = acc + b_ref[:, cs]

    y = y_ref[...]
    mu = jnp.mean(y, axis=-1, keepdims=True)
    yc = y - mu
    var = jnp.mean(yc * yc, axis=-1, keepdims=True)
    c_ref[...] = _silu(yc * lax.rsqrt(var + NORM_EPS) * lng_ref[...] + lnb_ref[...])

    tail = xp_ref[tt:tt + head, :]

    @pl.when(i == nt - 1)
    def _():
        nbuf_ref[0] = tail[off:, :]

    xp_ref[0:head, :] = tail


def _conv_branch(proj, buf, w, b, lng, lnb, *, bsz, seq, tt):
    nt = seq // tt
    ca = COL_GLU // CONV_DIM
    row = lambda b_, i: b_ * nt + i
    vec = lambda v: v.reshape(1, CONV_DIM)
    return pl.pallas_call(
        functools.partial(_conv_kernel, tt=tt, nt=nt),
        grid=(bsz, nt),
        in_specs=[
            pl.BlockSpec((tt, CONV_DIM), lambda b_, i: (row(b_, i), ca)),
            pl.BlockSpec((tt, CONV_DIM), lambda b_, i: (row(b_, i), ca + 1)),
            pl.BlockSpec((1, CONV_K - 1, CONV_DIM), lambda b_, i: (b_, 0, 0)),
            pl.BlockSpec((CONV_K, CONV_DIM), lambda b_, i: (0, 0)),
            pl.BlockSpec((1, CONV_DIM), lambda b_, i: (0, 0)),
            pl.BlockSpec((1, CONV_DIM), lambda b_, i: (0, 0)),
            pl.BlockSpec((1, CONV_DIM), lambda b_, i: (0, 0)),
        ],
        out_specs=[
            pl.BlockSpec((tt, CONV_DIM), lambda b_, i: (row(b_, i), 0)),
            pl.BlockSpec((1, CONV_K - 1, CONV_DIM), lambda b_, i: (b_, 0, 0)),
        ],
        out_shape=[
            jax.ShapeDtypeStruct((bsz * seq, CONV_DIM), F32),
            jax.ShapeDtypeStruct((bsz, CONV_K - 1, CONV_DIM), F32),
        ],
        scratch_shapes=[
            pltpu.VMEM((tt + 32, CONV_DIM), F32),
            pltpu.VMEM((SUBLANES - 1, tt + 32 - SUBLANES, CONV_DIM), F32),
            pltpu.VMEM((tt, CONV_DIM), F32),
        ],
        compiler_params=_cparams("parallel", "arbitrary"),
        name="conv_branch",
    )(proj, proj, buf, w, vec(b), vec(lng), vec(lnb))


def _dn_prep_kernel(x_ref, ab_ref, buf_ref, w_ref, alog_ref, dtb_ref, o_ref, g_ref, beta_ref, nbuf_ref, xp_ref, *, tt, nt):
    i = pl.program_id(1)
    head = SUBLANES
    off = head - (DN_CONV_K - 1)

    @pl.when(i == 0)
    def _():
        xp_ref[0:off, :] = jnp.zeros((off, 3 * DN_DIM), F32)
        xp_ref[off:head, :] = buf_ref[0]

    xp_ref[head:head + tt, :] = x_ref[...]

    for blk in range(3 * DN_HEADS):
        cs = slice(blk * DN_HEAD_DIM, (blk + 1) * DN_HEAD_DIM)
        acc = jnp.zeros((tt, DN_HEAD_DIM), F32)
        for j in range(DN_CONV_K):
            acc = acc + w_ref[j:j + 1, cs] * xp_ref[pl.ds(off + j, tt), cs]
        y = _silu(acc)
        if blk < 2 * DN_HEADS:
            y = y * lax.rsqrt(jnp.sum(y * y, axis=-1, keepdims=True) + NORM_EPS)
            if blk < DN_HEADS:
                y = y * (DN_HEAD_DIM ** -0.5)
        o_ref[:, cs] = y

    ab = ab_ref[...]
    a_raw = ab[:, 0:DN_HEADS]
    b_raw = ab[:, DN_HEADS:2 * DN_HEADS]
    sp_in = a_raw + dtb_ref[...]
    softplus = jnp.maximum(sp_in, 0.0) + jnp.log1p(jnp.exp(-jnp.abs(sp_in)))
    g_ref[...] = -jnp.exp(alog_ref[...]) * softplus
    beta_ref[...] = jax.nn.sigmoid(b_raw)

    tail = xp_ref[tt:tt + head, :]

    @pl.when(i == nt - 1)
    def _():
        nbuf_ref[0] = tail[off:, :]

    xp_ref[0:head, :] = tail


def _dn_prep(proj, buf, w, a_log, dt_bias, *, bsz, seq, tt):
    nt = seq // tt
    dq = 3 * DN_DIM
    row = lambda b_, i: b_ * nt + i
    return pl.pallas_call(
        functools.partial(_dn_prep_kernel, tt=tt, nt=nt),
        grid=(bsz, nt),
        in_specs=[
            pl.BlockSpec((tt, dq), lambda b_, i: (row(b_, i), COL_QKV // dq)),
            pl.BlockSpec((tt, LANES), lambda b_, i: (row(b_, i), COL_AB // LANES)),
            pl.BlockSpec((1, DN_CONV_K - 1, dq), lambda b_, i: (b_, 0, 0)),
            pl.BlockSpec((DN_CONV_K, dq), lambda b_, i: (0, 0)),
            pl.BlockSpec((1, DN_HEADS), lambda b_, i: (0, 0)),
            pl.BlockSpec((1, DN_HEADS), lambda b_, i: (0, 0)),
        ],
        out_specs=[
            pl.BlockSpec((tt, dq), lambda b_, i: (row(b_, i), 0)),
            pl.BlockSpec((tt, DN_HEADS), lambda b_, i: (row(b_, i), 0)),
            pl.BlockSpec((tt, DN_HEADS), lambda b_, i: (row(b_, i), 0)),
            pl.BlockSpec((1, DN_CONV_K - 1, dq), lambda b_, i: (b_, 0, 0)),
        ],
        out_shape=[
            jax.ShapeDtypeStruct((bsz * seq, dq), F32),
            jax.ShapeDtypeStruct((bsz * seq, DN_HEADS), F32),
            jax.ShapeDtypeStruct((bsz * seq, DN_HEADS), F32),
            jax.ShapeDtypeStruct((bsz, DN_CONV_K - 1, dq), F32),
        ],
        scratch_shapes=[pltpu.VMEM((tt + SUBLANES, dq), F32)],
        compiler_params=_cparams("parallel", "arbitrary"),
        name="dn_prep",
    )(proj, proj, buf, w, a_log.reshape(1, DN_HEADS), dt_bias.reshape(1, DN_HEADS))


def _dot_split(a, b):
    a_hi = a.astype(BF16)
    b_hi = b.astype(BF16)
    a_lo = (a - a_hi.astype(F32)).astype(BF16)
    b_lo = (b - b_hi.astype(F32)).astype(BF16)
    return _dot(a_hi, b_hi) + (_dot(a_hi, b_lo) + _dot(a_lo, b_hi))


def _run_interleaved(gens):
    while gens:
        alive = []
        for g in gens:
            try:
                next(g)
                alive.append(g)
            except StopIteration:
                pass
        gens = alive


def _unit_lower_inverse(a, masks):
    eye, diag_block, off_blocks = masks
    ad = jnp.where(diag_block, a, 0.0)
    a2 = _dot_split(ad, ad)
    yield
    a4 = _dot_split(a2, a2)
    x = _dot_split(eye - ad, eye + a2)
    yield
    x = _dot_split(x, eye + a4)
    yield
    for off_block in off_blocks:
        y = _dot(x.astype(BF16), jnp.where(off_block, a, 0.0).astype(BF16))
        yield
        x = x - _dot(y.astype(BF16), x.astype(BF16))
        yield
    return x


def _inverse_masks(c):
    ri = lax.broadcasted_iota(jnp.int32, (c, c), 0)
    ci = lax.broadcasted_iota(jnp.int32, (c, c), 1)
    eye = (ri == ci).astype(F32)
    diag_block = (ri >> 3) == (ci >> 3)
    off_blocks = []
    shift = 3
    while (1 << shift) < c:
        off_blocks.append(((ri >> (shift + 1)) == (ci >> (shift + 1))) & ((ri >> shift) != (ci >> shift)))
        shift += 1
    return eye, diag_block, off_blocks


def _dn_head(q, k, v, z, gcol, beta, grow, s, ng, tri, inv_masks, c):
    incl, strict, upper = tri
    gc_col = jnp.sum(jnp.where(incl, grow, 0.0), axis=1, keepdims=True)
    gc_row = jnp.sum(jnp.where(upper, gcol, 0.0), axis=0, keepdims=True)
    decay_incl = jnp.exp(jnp.where(incl, gc_col - gc_row, -jnp.inf))
    decay_strict = jnp.where(strict, decay_incl, 0.0)

    kb = k * beta
    kq = _dot_nt(jnp.concatenate([kb, q], axis=0).astype(BF16), k.astype(BF16))
    yield
    t_inv = yield from _unit_lower_inverse(kq[:c] * decay_strict, inv_masks)
    e_gc = jnp.exp(gc_col)
    rhs = jnp.concatenate([v * beta, kb * e_gc], axis=1)
    sol = _dot(t_inv.astype(BF16), rhs.astype(BF16))
    yield
    value = sol[:, :DN_HEAD_DIM]
    k_cum = sol[:, DN_HEAD_DIM:]
    qk = kq[c:] * decay_incl
    gc_last = gc_col[c - 1:c, :]
    k_dec = k * jnp.exp(gc_last - gc_col)

    ks = _dot(jnp.concatenate([k_cum, q * e_gc], axis=0).astype(BF16), s.astype(BF16))
    yield
    u16 = (value - ks[:c]).astype(BF16)
    o = ks[c:] + _dot(qk.astype(BF16), u16)
    s_new = s * jnp.exp(gc_last) + _dot_tn(k_dec.astype(BF16), u16)
    return _rms(o, ng) * _silu(z), s_new


def _dn_chunk_kernel(q_ref, k_ref, v_ref, z_ref, g_ref, beta_ref, grow_ref, s0_ref, ng_ref, o_ref, s_ref, *, c):
    @pl.when(pl.program_id(1) == 0)
    def _():
        s_ref[...] = s0_ref[...]

    ri = lax.broadcasted_iota(jnp.int32, (c, c), 0)
    ci = lax.broadcasted_iota(jnp.int32, (c, c), 1)
    tri = (ri >= ci, ri > ci, ri <= ci)
    inv_masks = _inverse_masks(c)
    ng = ng_ref[...]
    d = DN_HEAD_DIM

    def head(h):
        cs = slice(h * d, (h + 1) * d)
        o, s_new = yield from _dn_head(q_ref[:, cs], k_ref[:, cs], v_ref[:, cs], z_ref[:, cs], g_ref[:, h:h + 1],
                                       beta_ref[:, h:h + 1], grow_ref[0, 0, h:h + 1, :], s_ref[0, h], ng, tri,
                                       inv_masks, c)
        o_ref[:, cs] = o
        s_ref[0, h] = s_new

    _run_interleaved([head(h) for h in range(DN_HEADS)])


def _dn_chunk(qkvc, proj, g, beta, s0, norm_g, *, bsz, seq, c):
    n = seq // c
    h_ = DN_HEADS
    d = DN_HEAD_DIM
    grow = jnp.transpose(g.reshape(bsz, n, c, h_), (0, 1, 3, 2))
    row = lambda b_, n_: (b_ * n + n_)
    return pl.pallas_call(
        functools.partial(_dn_chunk_kernel, c=c),
        grid=(bsz, n),
        in_specs=[
            pl.BlockSpec((c, DN_DIM), lambda b_, n_: (row(b_, n_), 0)),
            pl.BlockSpec((c, DN_DIM), lambda b_, n_: (row(b_, n_), 1)),
            pl.BlockSpec((c, DN_DIM), lambda b_, n_: (row(b_, n_), 2)),
            pl.BlockSpec((c, DN_DIM), lambda b_, n_: (row(b_, n_), COL_Z // DN_DIM)),
            pl.BlockSpec((c, h_), lambda b_, n_: (row(b_, n_), 0)),
            pl.BlockSpec((c, h_), lambda b_, n_: (row(b_, n_), 0)),
            pl.BlockSpec((1, 1, h_, c), lambda b_, n_: (b_, n_, 0, 0)),
            pl.BlockSpec((1, h_, d, d), lambda b_, n_: (b_, 0, 0, 0)),
            pl.BlockSpec((1, d), lambda b_, n_: (0, 0)),
        ],
        out_specs=[
            pl.BlockSpec((c, DN_DIM), lambda b_, n_: (row(b_, n_), 0)),
            pl.BlockSpec((1, h_, d, d), lambda b_, n_: (b_, 0, 0, 0)),
        ],
        out_shape=[
            jax.ShapeDtypeStruct((bsz * seq, DN_DIM), F32),
            jax.ShapeDtypeStruct((bsz, h_, d, d), F32),
        ],
        compiler_params=_cparams("parallel", "arbitrary"),
        name="dn_chunk",
    )(qkvc, qkvc, qkvc, proj, g, beta, grow, s0, norm_g.reshape(1, d))


def _mem_attn_kernel(q_ref, k_ref, v_ref, o_ref):
    for h in range(XA_HEADS):
        cs = slice(h * XA_HEAD_DIM, (h + 1) * XA_HEAD_DIM)
        s = _dot_nt(q_ref[:, cs].astype(BF16), k_ref[0, :, cs].astype(BF16)) * (XA_HEAD_DIM ** -0.5)
        e = jnp.exp(s - jnp.max(s, axis=-1, keepdims=True))
        p = e / jnp.sum(e, axis=-1, keepdims=True)
        o_ref[:, cs] = _dot(p.astype(BF16), v_ref[0, :, cs].astype(BF16))


def _mem_attention(proj, mem_k, mem_v, *, bsz, seq, tl):
    nl = seq // tl
    m = mem_k.shape[1]
    return pl.pallas_call(
        _mem_attn_kernel,
        grid=(bsz, nl),
        in_specs=[
            pl.BlockSpec((tl, XA_DIM), lambda b_, i: (b_ * nl + i, COL_XQ // XA_DIM)),
            pl.BlockSpec((1, m, XA_DIM), lambda b_, i: (b_, 0, 0)),
            pl.BlockSpec((1, m, XA_DIM), lambda b_, i: (b_, 0, 0)),
        ],
        out_specs=pl.BlockSpec((tl, XA_DIM), lambda b_, i: (b_ * nl + i, 0)),
        out_shape=jax.ShapeDtypeStruct((bsz * seq, XA_DIM), F32),
        compiler_params=_cparams("parallel", "arbitrary"),
        name="mem_attention",
    )(proj, mem_k, mem_v)


def _merge_kernel(c_ref, o_ref, a_ref, gc_ref, gd_ref, gx_ref, wc_ref, wd_ref, wx_ref, m_ref):
    yc = _dot(c_ref[...].astype(BF16), wc_ref[...])
    yd = _dot(o_ref[...].astype(BF16), wd_ref[...])
    yx = _dot(a_ref[...].astype(BF16), wx_ref[...])
    m_ref[...] = (jax.nn.sigmoid(gc_ref[...]) * yc + jax.nn.sigmoid(gd_ref[...]) * yd
                  + jax.nn.sigmoid(gx_ref[...]) * yx).astype(BF16)


def _merge(c, o, a, proj, wc, wd, wx, *, tm, tn):
    t = c.shape[0]
    dm = wc.shape[1]
    gcol = COL_GATES // tn
    gstep = dm // tn
    return pl.pallas_call(
        _merge_kernel,
        grid=(t // tm, dm // tn),
        in_specs=[
            pl.BlockSpec((tm, CONV_DIM), lambda i, j: (i, 0)),
            pl.BlockSpec((tm, DN_DIM), lambda i, j: (i, 0)),
            pl.BlockSpec((tm, XA_DIM), lambda i, j: (i, 0)),
            pl.BlockSpec((tm, tn), lambda i, j: (i, gcol + j)),
            pl.BlockSpec((tm, tn), lambda i, j: (i, gcol + gstep + j)),
            pl.BlockSpec((tm, tn), lambda i, j: (i, gcol + 2 * gstep + j)),
            pl.BlockSpec((CONV_DIM, tn), lambda i, j: (0, j)),
            pl.BlockSpec((DN_DIM, tn), lambda i, j: (0, j)),
            pl.BlockSpec((XA_DIM, tn), lambda i, j: (0, j)),
        ],
        out_specs=pl.BlockSpec((tm, tn), lambda i, j: (i, j)),
        out_shape=jax.ShapeDtypeStruct((t, dm), BF16),
        compiler_params=_cparams("parallel", "arbitrary"),
        name="merge",
    )(c, o, a, proj, proj, proj, wc, wd, wx)


def _out_proj_kernel(m_ref, w_ref, x_ref, o_ref):
    o_ref[...] = x_ref[...] + _dot(m_ref[...], w_ref[...])


def _out_proj(merged, w, x, *, tm, tn):
    t, dm = x.shape
    return pl.pallas_call(
        _out_proj_kernel,
        grid=(t // tm, dm // tn),
        in_specs=[
            pl.BlockSpec((tm, dm), lambda i, j: (i, 0)),
            pl.BlockSpec((dm, tn), lambda i, j: (0, j)),
            pl.BlockSpec((tm, tn), lambda i, j: (i, j)),
        ],
        out_specs=pl.BlockSpec((tm, tn), lambda i, j: (i, j)),
        out_shape=jax.ShapeDtypeStruct((t, dm), F32),
        compiler_params=_cparams("parallel", "arbitrary"),
        name="out_proj",
    )(merged, w, x)


def _top16(s, tm):
    key = lax.broadcasted_iota(jnp.int32, s.shape, 0).astype(F32)
    work = s
    rank = jnp.full(s.shape, float(PEER_TOPK), F32)
    vals = []
    for a in range(PEER_TOPK):
        m = jnp.max(work, axis=0, keepdims=True)
        idx = jnp.min(jnp.where(work == m, key, float(PEER_N_KEYS)), axis=0, keepdims=True)
        sel = key == idx
        rank = jnp.where(sel, float(a), rank)
        work = jnp.where(sel, -jnp.inf, work)
        vals.append(m)
    return vals, rank


def _peer_route_kernel(q_ref, k1_ref, k2_ref, e1_ref, e2_ref, r2_ref, nrow_ref, *, tm):
    half = PEER_DK // 2
    q = q_ref[...]
    s1 = _dot_nt(k1_ref[...].astype(BF16), q[:, :half].astype(BF16))
    s2 = _dot_nt(k2_ref[...].astype(BF16), q[:, half:].astype(BF16))
    t1, rank1 = _top16(s1, tm)
    t2, rank2 = _top16(s2, tm)

    k = PEER_TOPK
    cands = [(a, b) for a in range(k) for b in range(k) if (a + 1) * (b + 1) <= k]
    no_pos = float(k * k)
    work, pos = [], []
    for g in range(0, len(cands), SUBLANES):
        group = cands[g:g + SUBLANES]
        pad = SUBLANES - len(group)
        work.append(jnp.concatenate([t1[a] + t2[b] for a, b in group]
                                    + [jnp.full((1, tm), -jnp.inf, F32)] * pad, axis=0))
        pos.append(jnp.concatenate([jnp.full((1, tm), float(a * k + b), F32) for a, b in group]
                                   + [jnp.full((1, tm), no_pos, F32)] * pad, axis=0))
    row_id = lax.broadcasted_iota(jnp.int32, (k, tm), 0).astype(F32)
    n_won = jnp.zeros((k, tm), F32)
    tops = []
    for _ in range(k):
        m = functools.reduce(jnp.maximum, work)
        m = jnp.max(m, axis=0, keepdims=True)
        p = functools.reduce(jnp.minimum, [jnp.where(w_ == m, p_, no_pos) for w_, p_ in zip(work, pos)])
        p = jnp.min(p, axis=0, keepdims=True)
        work = [jnp.where(p_ == p, -jnp.inf, w_) for w_, p_ in zip(work, pos)]
        n_won = n_won + (row_id == jnp.floor(p * (1.0 / k))).astype(F32)
        tops.append(m)
    zsum = jnp.zeros((1, tm), F32)
    for m in tops:
        zsum = zsum + jnp.exp(m - tops[0])
    nrow = jnp.zeros(s1.shape, F32)
    for a in range(k):
        nrow = jnp.where(rank1 == float(a), n_won[a:a + 1, :], nrow)

    e1_ref[0] = jnp.exp(s1 - t1[0]) / zsum
    e2_ref[0] = jnp.exp(s2 - t2[0])
    r2_ref[0] = rank2
    nrow_ref[0] = nrow


def _peer_route(q, k1, k2, *, tm):
    t = q.shape[0]
    nk = PEER_N_KEYS
    out = jax.ShapeDtypeStruct((PEER_HEADS, nk, t), F32)
    spec = pl.BlockSpec((1, nk, tm), lambda i, h: (h, 0, i))
    return pl.pallas_call(
        functools.partial(_peer_route_kernel, tm=tm),
        grid=(t // tm, PEER_HEADS),
        in_specs=[
            pl.BlockSpec((tm, PEER_DK), lambda i, h: (i, h)),
            pl.BlockSpec((nk, PEER_DK // 2), lambda i, h: (0, 0)),
            pl.BlockSpec((nk, PEER_DK // 2), lambda i, h: (0, 0)),
        ],
        out_specs=[spec, spec, spec, spec],
        out_shape=[out, out, out, out],
        compiler_params=_cparams("parallel", "arbitrary"),
        name="peer_route",
    )(q, k1, k2)


def _gelu_tanh(x):
    c = 0.7978845608028654
    return x * (0.5 + 0.5 * jnp.tanh(x * (c + (c * 0.044715) * (x * x))))


def _peer_dense_kernel(x_ref, gn_ref, gf_ref, e1_ref, nrow_ref, e2_ref, r2_ref, u_ref, v_ref, y_ref,
                       h_ref, s_ref, p_ref, *, tm, eb, nj):
    j = pl.program_id(1)
    nk = PEER_N_KEYS

    @pl.when(j == 0)
    def _():
        h_ref[...] = _rms(x_ref[...], gn_ref[...]).astype(BF16)
        y_ref[...] = jnp.zeros_like(y_ref)

    s_ref[...] = _dot_nt(u_ref[...], h_ref[...])
    for bi in range(eb):
        for th in range(tm // LANES):
            ts = slice(th * LANES, (th + 1) * LANES)
            w = jnp.zeros((nk, LANES), F32)
            for h in range(PEER_HEADS):
                sel = r2_ref[h, :, ts] < nrow_ref[h, bi:bi + 1, ts]
                w = jnp.where(sel, w + e1_ref[h, bi:bi + 1, ts] * e2_ref[h, :, ts], w)
            act = _gelu_tanh(s_ref[bi * nk:(bi + 1) * nk, ts])
            p_ref[ts, bi * nk:(bi + 1) * nk] = (w * act).T.astype(BF16)
    y_ref[...] += _dot(p_ref[...], v_ref[...])

    @pl.when(j == nj - 1)
    def _():
        y_ref[...] = _rms(x_ref[...] + y_ref[...], gf_ref[...])


def _peer_dense(x, g_ffn, g_final, e1, e2, r2, nrow, u16, v16, *, tm, eb):
    t, dm = x.shape
    nk = PEER_N_KEYS
    ne = u16.shape[0]
    nj = ne // (eb * nk)
    return pl.pallas_call(
        functools.partial(_peer_dense_kernel, tm=tm, eb=eb, nj=nj),
        grid=(t // tm, nj),
        in_specs=[
            pl.BlockSpec((tm, dm), lambda i, j: (i, 0)),
            pl.BlockSpec((1, dm), lambda i, j: (0, 0)),
            pl.BlockSpec((1, dm), lambda i, j: (0, 0)),
            pl.BlockSpec((PEER_HEADS, eb, tm), lambda i, j: (0, j, i)),
            pl.BlockSpec((PEER_HEADS, eb, tm), lambda i, j: (0, j, i)),
            pl.BlockSpec((PEER_HEADS, nk, tm), lambda i, j: (0, 0, i)),
            pl.BlockSpec((PEER_HEADS, nk, tm), lambda i, j: (0, 0, i)),
            pl.BlockSpec((eb * nk, dm), lambda i, j: (j, 0)),
            pl.BlockSpec((eb * nk, dm), lambda i, j: (j, 0)),
        ],
        out_specs=pl.BlockSpec((tm, dm), lambda i, j: (i, 0)),
        out_shape=jax.ShapeDtypeStruct((t, dm), F32),
        scratch_shapes=[
            pltpu.VMEM((tm, dm), BF16),
            pltpu.VMEM((eb * nk, tm), F32),
            pltpu.VMEM((tm, eb * nk), BF16),
        ],
        compiler_params=_cparams("parallel", "arbitrary"),
        name="peer_dense",
    )(x, g_ffn.reshape(1, dm), g_final.reshape(1, dm), e1, nrow, e2, r2, u16, v16)


def _arrange_w_in(w_in):
    d = w_in.shape[0]
    sizes = (CONV_DIM, CONV_DIM, 3 * DN_DIM, DN_DIM, DN_HEADS, DN_HEADS, XA_DIM, 2048, 2048, 2048)
    offs = [0]
    for s in sizes:
        offs.append(offs[-1] + s)
    glu = w_in[:, offs[0]:offs[2]]
    qkv = w_in[:, offs[2]:offs[3]]
    z = w_in[:, offs[3]:offs[4]]
    ab = w_in[:, offs[4]:offs[6]]
    xq = w_in[:, offs[6]:offs[7]]
    gates = w_in[:, offs[7]:offs[10]]
    pad = jnp.zeros((d, PROJ_COLS - COL_AB - 2 * DN_HEADS), w_in.dtype)
    return jnp.concatenate([qkv, glu, z, gates, xq, ab, pad], axis=1).astype(BF16)


def _layer(x, mem_k, mem_v, conv_buf, qkv_buf, delta_state, w, *, chunk, tiles, g_final):
    bsz, seq, dm = x.shape
    t = bsz * seq
    assert seq % chunk == 0
    tiles = {name: min(size, t) for name, size in tiles.items()}
    x2d = x.reshape(t, dm)
    proj = _norm_matmul(x2d, w["norm_mix"], w["w_in"], tm=tiles["tm_proj"], tn=PROJ_TN, name="in_proj")
    c, new_conv = _conv_branch(proj, conv_buf, w["conv_dw"], w["conv_dw_b"], w["conv_ln_g"], w["conv_ln_b"],
                               bsz=bsz, seq=seq, tt=tiles["tt_conv"])
    qkvc, g, beta, new_qkv = _dn_prep(proj, qkv_buf, w["dn_conv_w"], w["dn_a_log"], w["dn_dt_bias"],
                                      bsz=bsz, seq=seq, tt=tiles["tt_dn"])
    o, new_state = _dn_chunk(qkvc, proj, g, beta, delta_state, w["dn_norm_g"], bsz=bsz, seq=seq, c=chunk)
    a = _mem_attention(proj, mem_k, mem_v, bsz=bsz, seq=seq, tl=tiles["tl_attn"])
    merged = _merge(c, o, a, proj, w["w_conv_out"], w["w_dn_out"], w["w_xa_out"], tm=tiles["tm_merge"], tn=512)
    x2 = _out_proj(merged, w["w_out"], x2d, tm=tiles["tm_merge"], tn=1024)
    q = _norm_matmul(x2, w["norm_ffn"], w["w_peer_q"], tm=tiles["tm_merge"], tn=1024, name="peer_q")
    e1, e2, r2, nrow = _peer_route(q, w["peer_keys_1"], w["peer_keys_2"], tm=LANES)
    y = _peer_dense(x2, w["norm_ffn"], g_final, e1, e2, r2, nrow, w["peer_u"], w["peer_v"],
                    tm=tiles["tm_peer"], eb=8)
    return y.reshape(bsz, seq, dm), new_conv, new_qkv, new_state


PROMPT_TILES = dict(tm_proj=1024, tt_conv=256, tt_dn=128, tl_attn=256, tm_merge=512, tm_peer=512)
SAMPLE_TILES = dict(tm_proj=1024, tt_conv=8, tt_dn=8, tl_attn=8, tm_merge=512, tm_peer=512)


def kernel(x_prompt, x_sample, mem_prompt, cache_mem_k, cache_mem_v, state_conv, state_qkv_conv, state_delta, norm_mix, norm_mem, w_in, conv_dw, conv_dw_b, conv_ln_g, conv_ln_b, w_conv_out, dn_conv_w, dn_a_log, dn_dt_bias, dn_norm_g, w_dn_out, w_mem_kv, w_xa_out, w_out, norm_ffn, w_peer_q, peer_keys_1, peer_keys_2, peer_u, peer_v, norm_final):
    depth = w_in.shape[0]
    assert depth == 1, "the final norm is fused into the last layer's PEER kernel; one layer is supported"
    bp, sp, dm = x_prompt.shape
    bs, ss, _ = x_sample.shape
    n_mem = mem_prompt.shape[1]
    def layer0(a):
        return a.reshape(a.shape[1:])

    w = dict(
        norm_mix=layer0(norm_mix), w_in=_arrange_w_in(layer0(w_in)), conv_dw=layer0(conv_dw),
        conv_dw_b=layer0(conv_dw_b), conv_ln_g=layer0(conv_ln_g), conv_ln_b=layer0(conv_ln_b),
        w_conv_out=layer0(w_conv_out).astype(BF16), dn_conv_w=layer0(dn_conv_w), dn_a_log=layer0(dn_a_log),
        dn_dt_bias=layer0(dn_dt_bias), dn_norm_g=layer0(dn_norm_g), w_dn_out=layer0(w_dn_out).astype(BF16),
        w_xa_out=layer0(w_xa_out).astype(BF16), w_out=layer0(w_out).astype(BF16), norm_ffn=layer0(norm_ffn),
        w_peer_q=layer0(w_peer_q).astype(BF16), peer_keys_1=layer0(peer_keys_1), peer_keys_2=layer0(peer_keys_2),
        peer_u=layer0(peer_u).astype(BF16), peer_v=layer0(peer_v).astype(BF16),
    )
    mem2d = mem_prompt.reshape(bp * n_mem, dm)
    wkv = layer0(w_mem_kv).astype(BF16)
    mk = _norm_matmul(mem2d, layer0(norm_mem), wkv[:, :XA_DIM], tm=bp * n_mem, tn=512, name="mem_k")
    mv = _norm_matmul(mem2d, layer0(norm_mem), wkv[:, XA_DIM:], tm=bp * n_mem, tn=512, name="mem_v")
    mk = mk.reshape(bp, n_mem, XA_DIM)
    mv = mv.reshape(bp, n_mem, XA_DIM)
    yp, conv_p, qkv_p, delta_p = _layer(
        x_prompt, mk, mv,
        jnp.zeros((bp, CONV_K - 1, CONV_DIM), F32),
        jnp.zeros((bp, DN_CONV_K - 1, 3 * DN_DIM), F32),
        jnp.zeros((bp, DN_HEADS, DN_HEAD_DIM, DN_HEAD_DIM), F32),
        w, chunk=min(DN_CHUNK, sp), tiles=PROMPT_TILES, g_final=norm_final)
    ys, conv_s, qkv_s, delta_s = _layer(
        x_sample, cache_mem_k.reshape(bs, n_mem, XA_DIM), cache_mem_v.reshape(bs, n_mem, XA_DIM),
        layer0(state_conv), layer0(state_qkv_conv), layer0(state_delta),
        w, chunk=min(DN_CHUNK, ss), tiles=SAMPLE_TILES, g_final=norm_final)
    kv_shape = (1, bp, n_mem, XA_HEADS, XA_HEAD_DIM)
    return (yp, ys, mk.reshape(kv_shape), mv.reshape(kv_shape), conv_p[None], qkv_p[None], delta_p[None],
            conv_s[None], qkv_s[None], delta_s[None])
```

```python
import functools

import jax
import jax.numpy as jnp
from jax import lax
from jax.experimental import pallas as pl
from jax.experimental.pallas import tpu as pltpu

F32 = jnp.float32
BF16 = jnp.bfloat16
HIGHEST = lax.Precision.HIGHEST

NORM_EPS = 1e-6
LANES = 128
SUBLANES = 8
VMEM_LIMIT_BYTES = 56 * 1024 * 1024

CONV_DIM = 1024
CONV_K = 31
DN_HEADS = 16
DN_HEAD_DIM = 128
DN_DIM = DN_HEADS * DN_HEAD_DIM
DN_CONV_K = 4
DN_CHUNK = 64
XA_HEADS = 4
XA_HEAD_DIM = 256
XA_DIM = XA_HEADS * XA_HEAD_DIM
PEER_HEADS = 8
PEER_N_KEYS = 128
PEER_DK = 256
PEER_TOPK = 16

COL_QKV = 0
COL_GLU = 3 * DN_DIM
COL_Z = COL_GLU + 2 * CONV_DIM
COL_GATES = COL_Z + DN_DIM
COL_XQ = COL_GATES + 3 * 2048
COL_AB = COL_XQ + XA_DIM
PROJ_TN = 1280
PROJ_COLS = 14 * PROJ_TN


def _cparams(*dims):
    return pltpu.CompilerParams(dimension_semantics=dims, vmem_limit_bytes=VMEM_LIMIT_BYTES)


def _dot(a, b, precision=None):
    return jnp.dot(a, b, preferred_element_type=F32, precision=precision)


def _dot_nt(a, b):
    return lax.dot_general(a, b, (((1,), (1,)), ((), ())), preferred_element_type=F32)


def _dot_tn(a, b):
    return lax.dot_general(a, b, (((0,), (0,)), ((), ())), preferred_element_type=F32)


def _rms(x, g):
    return x * lax.rsqrt(jnp.mean(x * x, axis=-1, keepdims=True) + NORM_EPS) * g


def _silu(x):
    return x * jax.nn.sigmoid(x)


def _norm_matmul_kernel(x_ref, g_ref, w_ref, o_ref, h_ref):
    @pl.when(pl.program_id(1) == 0)
    def _():
        h_ref[...] = _rms(x_ref[...], g_ref[...]).astype(BF16)

    o_ref[...] = _dot(h_ref[...], w_ref[...])


def _norm_matmul(x, g, w, *, tm, tn, name):
    t, d = x.shape
    n = w.shape[1]
    return pl.pallas_call(
        _norm_matmul_kernel,
        grid=(t // tm, n // tn),
        in_specs=[
            pl.BlockSpec((tm, d), lambda i, j: (i, 0)),
            pl.BlockSpec((1, d), lambda i, j: (0, 0)),
            pl.BlockSpec((d, tn), lambda i, j: (0, j)),
        ],
        out_specs=pl.BlockSpec((tm, tn), lambda i, j: (i, j)),
        out_shape=jax.ShapeDtypeStruct((t, n), F32),
        scratch_shapes=[pltpu.VMEM((tm, d), BF16)],
        compiler_params=_cparams("parallel", "arbitrary"),
        name=name,
    )(x, g.reshape(1, d), w)


def _conv_kernel(a_ref, g_ref, buf_ref, w_ref, b_ref, lng_ref, lnb_ref, c_ref, nbuf_ref, xp_ref, sh_ref, y_ref,
                 *, tt, nt, bb):
    i = pl.program_id(1)
    head = 32
    off = head - (CONV_K - 1)

    for bi in range(bb):
        rows_b = slice(bi * tt, (bi + 1) * tt)

        def load_history(bi=bi):
            xp_ref[0:off, :] = jnp.zeros((off, CONV_DIM), F32)
            xp_ref[off:head, :] = buf_ref[bi]

        if nt == 1:
            load_history()
        else:
            pl.when(i == 0)(load_history)

        xp_ref[head:head + tt, :] = a_ref[rows_b, :] * jax.nn.sigmoid(g_ref[rows_b, :])

        span = tt + head - SUBLANES
        for s in range(1, SUBLANES):
            sh_ref[s - 1] = xp_ref[pl.ds(s, span), :]

        rt = min(32, tt)
        cw = 512
        for r in range(tt // rt):
            for cb in range(CONV_DIM // cw):
                cs = slice(cb * cw, (cb + 1) * cw)
                acc = jnp.zeros((rt, cw), F32)
                for j in range(CONV_K):
                    s = (off + j) % SUBLANES
                    rows = pl.ds(r * rt + off + j - s, rt)
                    tap = xp_ref[rows, cs] if s == 0 else sh_ref[s - 1, rows, cs]
                    acc = acc + w_ref[j:j + 1, cs] * tap
                y_ref[r * rt:(r + 1) * rt, cs] = acc + b_ref[:, cs]

        y = y_ref[...]
        mu = jnp.mean(y, axis=-1, keepdims=True)
        yc = y - mu
        var = jnp.mean(yc * yc, axis=-1, keepdims=True)
        c_ref[rows_b, :] = _silu(yc * lax.rsqrt(var + NORM_EPS) * lng_ref[...] + lnb_ref[...])

        tail = xp_ref[tt:tt + head, :]
        if nt == 1:
            nbuf_ref[bi] = tail[off:, :]
        else:
            def store_history(tail=tail, bi=bi):
                nbuf_ref[bi] = tail[off:, :]

            pl.when(i == nt - 1)(store_history)
            xp_ref[0:head, :] = tail


def _conv_branch(proj, buf, w, b, lng, lnb, *, bsz, seq, tt, bb):
    nt = seq // tt
    assert bb == 1 or nt == 1
    ca = COL_GLU // CONV_DIM
    row = lambda b_, i: b_ * nt + i
    vec = lambda v: v.reshape(1, CONV_DIM)
    return pl.pallas_call(
        functools.partial(_conv_kernel, tt=tt, nt=nt, bb=bb),
        grid=(bsz // bb, nt),
        in_specs=[
            pl.BlockSpec((bb * tt, CONV_DIM), lambda b_, i: (row(b_, i), ca)),
            pl.BlockSpec((bb * tt, CONV_DIM), lambda b_, i: (row(b_, i), ca + 1)),
            pl.BlockSpec((bb, CONV_K - 1, CONV_DIM), lambda b_, i: (b_, 0, 0)),
            pl.BlockSpec((CONV_K, CONV_DIM), lambda b_, i: (0, 0)),
            pl.BlockSpec((1, CONV_DIM), lambda b_, i: (0, 0)),
            pl.BlockSpec((1, CONV_DIM), lambda b_, i: (0, 0)),
            pl.BlockSpec((1, CONV_DIM), lambda b_, i: (0, 0)),
        ],
        out_specs=[
            pl.BlockSpec((bb * tt, CONV_DIM), lambda b_, i: (row(b_, i), 0)),
            pl.BlockSpec((bb, CONV_K - 1, CONV_DIM), lambda b_, i: (b_, 0, 0)),
        ],
        out_shape=[
            jax.ShapeDtypeStruct((bsz * seq, CONV_DIM), F32),
            jax.ShapeDtypeStruct((bsz, CONV_K - 1, CONV_DIM), F32),
        ],
        scratch_shapes=[
            pltpu.VMEM((tt + 32, CONV_DIM), F32),
            pltpu.VMEM((SUBLANES - 1, tt + 32 - SUBLANES, CONV_DIM), F32),
            pltpu.VMEM((tt, CONV_DIM), F32),
        ],
        compiler_params=_cparams("parallel", "arbitrary"),
        name="conv_branch",
    )(proj, proj, buf, w, vec(b), vec(lng), vec(lnb))


def _dn_prep_kernel(x_ref, ab_ref, buf_ref, w_ref, alog_ref, dtb_ref, o_ref, g_ref, beta_ref, nbuf_ref, xp_ref,
                    *, tt, nt, bb):
    i = pl.program_id(1)
    head = SUBLANES
    off = head - (DN_CONV_K - 1)

    for bi in range(bb):
        rows_b = slice(bi * tt, (bi + 1) * tt)

        def load_history(bi=bi):
            xp_ref[0:off, :] = jnp.zeros((off, 3 * DN_DIM), F32)
            xp_ref[off:head, :] = buf_ref[bi]

        if nt == 1:
            load_history()
        else:
            pl.when(i == 0)(load_history)

        xp_ref[head:head + tt, :] = x_ref[rows_b, :]

        for blk in range(3 * DN_HEADS):
            cs = slice(blk * DN_HEAD_DIM, (blk + 1) * DN_HEAD_DIM)
            acc = jnp.zeros((tt, DN_HEAD_DIM), F32)
            for j in range(DN_CONV_K):
                acc = acc + w_ref[j:j + 1, cs] * xp_ref[pl.ds(off + j, tt), cs]
            y = _silu(acc)
            if blk < 2 * DN_HEADS:
                y = y * lax.rsqrt(jnp.sum(y * y, axis=-1, keepdims=True) + NORM_EPS)
                if blk < DN_HEADS:
                    y = y * (DN_HEAD_DIM ** -0.5)
            o_ref[rows_b, cs] = y

        tail = xp_ref[tt:tt + head, :]
        if nt == 1:
            nbuf_ref[bi] = tail[off:, :]
        else:
            def store_history(tail=tail, bi=bi):
                nbuf_ref[bi] = tail[off:, :]

            pl.when(i == nt - 1)(store_history)
            xp_ref[0:head, :] = tail

    ab = ab_ref[...]
    a_raw = ab[:, 0:DN_HEADS]
    b_raw = ab[:, DN_HEADS:2 * DN_HEADS]
    sp_in = a_raw + dtb_ref[...]
    softplus = jnp.maximum(sp_in, 0.0) + jnp.log1p(jnp.exp(-jnp.abs(sp_in)))
    g_ref[...] = -jnp.exp(alog_ref[...]) * softplus
    beta_ref[...] = jax.nn.sigmoid(b_raw)


def _dn_prep(proj, buf, w, a_log, dt_bias, *, bsz, seq, tt, bb):
    nt = seq // tt
    assert bb == 1 or nt == 1
    dq = 3 * DN_DIM
    row = lambda b_, i: b_ * nt + i
    return pl.pallas_call(
        functools.partial(_dn_prep_kernel, tt=tt, nt=nt, bb=bb),
        grid=(bsz // bb, nt),
        in_specs=[
            pl.BlockSpec((bb * tt, dq), lambda b_, i: (row(b_, i), COL_QKV // dq)),
            pl.BlockSpec((bb * tt, LANES), lambda b_, i: (row(b_, i), COL_AB // LANES)),
            pl.BlockSpec((bb, DN_CONV_K - 1, dq), lambda b_, i: (b_, 0, 0)),
            pl.BlockSpec((DN_CONV_K, dq), lambda b_, i: (0, 0)),
            pl.BlockSpec((1, DN_HEADS), lambda b_, i: (0, 0)),
            pl.BlockSpec((1, DN_HEADS), lambda b_, i: (0, 0)),
        ],
        out_specs=[
            pl.BlockSpec((bb * tt, dq), lambda b_, i: (row(b_, i), 0)),
            pl.BlockSpec((bb * tt, DN_HEADS), lambda b_, i: (row(b_, i), 0)),
            pl.BlockSpec((bb * tt, DN_HEADS), lambda b_, i: (row(b_, i), 0)),
            pl.BlockSpec((bb, DN_CONV_K - 1, dq), lambda b_, i: (b_, 0, 0)),
        ],
        out_shape=[
            jax.ShapeDtypeStruct((bsz * seq, dq), F32),
            jax.ShapeDtypeStruct((bsz * seq, DN_HEADS), F32),
            jax.ShapeDtypeStruct((bsz * seq, DN_HEADS), F32),
            jax.ShapeDtypeStruct((bsz, DN_CONV_K - 1, dq), F32),
        ],
        scratch_shapes=[pltpu.VMEM((tt + SUBLANES, dq), F32)],
        compiler_params=_cparams("parallel", "arbitrary"),
        name="dn_prep",
    )(proj, proj, buf, w, a_log.reshape(1, DN_HEADS), dt_bias.reshape(1, DN_HEADS))


def _dot_split(a, b):
    a_hi = a.astype(BF16)
    b_hi = b.astype(BF16)
    a_lo = (a - a_hi.astype(F32)).astype(BF16)
    b_lo = (b - b_hi.astype(F32)).astype(BF16)
    return _dot(a_hi, b_hi) + (_dot(a_hi, b_lo) + _dot(a_lo, b_hi))


def _run_interleaved(gens):
    while gens:
        alive = []
        for g in gens:
            try:
                next(g)
                alive.append(g)
            except StopIteration:
                pass
        gens = alive


def _unit_lower_inverse(a, masks):
    eye, diag_block, off_blocks = masks
    ad = jnp.where(diag_block, a, 0.0)
    a2 = _dot_split(ad, ad)
    yield
    a4 = _dot_split(a2, a2)
    x = _dot_split(eye - ad, eye + a2)
    yield
    x = _dot_split(x, eye + a4)
    yield
    for off_block in off_blocks:
        y = _dot(x.astype(BF16), jnp.where(off_block, a, 0.0).astype(BF16))
        yield
        x = x - _dot(y.astype(BF16), x.astype(BF16))
        yield
    return x


def _inverse_masks(c):
    ri = lax.broadcasted_iota(jnp.int32, (c, c), 0)
    ci = lax.broadcasted_iota(jnp.int32, (c, c), 1)
    eye = (ri == ci).astype(F32)
    diag_block = (ri >> 3) == (ci >> 3)
    off_blocks = []
    shift = 3
    while (1 << shift) < c:
        off_blocks.append(((ri >> (shift + 1)) == (ci >> (shift + 1))) & ((ri >> shift) != (ci >> shift)))
        shift += 1
    return eye, diag_block, off_blocks


def _dn_head(q, k, v, z, gcol, beta, grow, s, ng, tri, inv_masks, c):
    incl, strict, upper = tri
    gc_col = jnp.sum(jnp.where(incl, grow, 0.0), axis=1, keepdims=True)
    gc_row = jnp.sum(jnp.where(upper, gcol, 0.0), axis=0, keepdims=True)
    decay_incl = jnp.exp(jnp.where(incl, gc_col - gc_row, -jnp.inf))
    decay_strict = jnp.where(strict, decay_incl, 0.0)

    kb = k * beta
    kq = _dot_nt(jnp.concatenate([kb, q], axis=0).astype(BF16), k.astype(BF16))
    yield
    t_inv = yield from _unit_lower_inverse(kq[:c] * decay_strict, inv_masks)
    e_gc = jnp.exp(gc_col)
    rhs = jnp.concatenate([v * beta, kb * e_gc], axis=1)
    sol = _dot(t_inv.astype(BF16), rhs.astype(BF16))
    yield
    value = sol[:, :DN_HEAD_DIM]
    k_cum = sol[:, DN_HEAD_DIM:]
    qk = kq[c:] * decay_incl
    gc_last = gc_col[c - 1:c, :]
    k_dec = k * jnp.exp(gc_last - gc_col)

    ks = _dot(jnp.concatenate([k_cum, q * e_gc], axis=0).astype(BF16), s.astype(BF16))
    yield
    u16 = (value - ks[:c]).astype(BF16)
    o = ks[c:] + _dot(qk.astype(BF16), u16)
    s_new = s * jnp.exp(gc_last) + _dot_tn(k_dec.astype(BF16), u16)
    return _rms(o, ng) * _silu(z), s_new


def _dn_chunk_kernel(q_ref, k_ref, v_ref, z_ref, g_ref, beta_ref, grow_ref, s0_ref, ng_ref, o_ref, s_ref, *, c):
    @pl.when(pl.program_id(1) == 0)
    def _():
        s_ref[...] = s0_ref[...]

    ri = lax.broadcasted_iota(jnp.int32, (c, c), 0)
    ci = lax.broadcasted_iota(jnp.int32, (c, c), 1)
    tri = (ri >= ci, ri > ci, ri <= ci)
    inv_masks = _inverse_masks(c)
    ng = ng_ref[...]
    d = DN_HEAD_DIM

    def head(h):
        cs = slice(h * d, (h + 1) * d)
        o, s_new = yield from _dn_head(q_ref[:, cs], k_ref[:, cs], v_ref[:, cs], z_ref[:, cs], g_ref[:, h:h + 1],
                                       beta_ref[:, h:h + 1], grow_ref[0, 0, h:h + 1, :], s_ref[0, h], ng, tri,
                                       inv_masks, c)
        o_ref[:, cs] = o
        s_ref[0, h] = s_new

    _run_interleaved([head(h) for h in range(DN_HEADS)])


def _dn_chunk(qkvc, proj, g, beta, s0, norm_g, *, bsz, seq, c):
    n = seq // c
    h_ = DN_HEADS
    d = DN_HEAD_DIM
    grow = jnp.transpose(g.reshape(bsz, n, c, h_), (0, 1, 3, 2))
    row = lambda b_, n_: (b_ * n + n_)
    return pl.pallas_call(
        functools.partial(_dn_chunk_kernel, c=c),
        grid=(bsz, n),
        in_specs=[
            pl.BlockSpec((c, DN_DIM), lambda b_, n_: (row(b_, n_), 0)),
            pl.BlockSpec((c, DN_DIM), lambda b_, n_: (row(b_, n_), 1)),
            pl.BlockSpec((c, DN_DIM), lambda b_, n_: (row(b_, n_), 2)),
            pl.BlockSpec((c, DN_DIM), lambda b_, n_: (row(b_, n_), COL_Z // DN_DIM)),
            pl.BlockSpec((c, h_), lambda b_, n_: (row(b_, n_), 0)),
            pl.BlockSpec((c, h_), lambda b_, n_: (row(b_, n_), 0)),
            pl.BlockSpec((1, 1, h_, c), lambda b_, n_: (b_, n_, 0, 0)),
            pl.BlockSpec((1, h_, d, d), lambda b_, n_: (b_, 0, 0, 0)),
            pl.BlockSpec((1, d), lambda b_, n_: (0, 0)),
        ],
        out_specs=[
            pl.BlockSpec((c, DN_DIM), lambda b_, n_: (row(b_, n_), 0)),
            pl.BlockSpec((1, h_, d, d), lambda b_, n_: (b_, 0, 0, 0)),
        ],
        out_shape=[
            jax.ShapeDtypeStruct((bsz * seq, DN_DIM), F32),
            jax.ShapeDtypeStruct((bsz, h_, d, d), F32),
        ],
        compiler_params=_cparams("parallel", "arbitrary"),
        name="dn_chunk",
    )(qkvc, qkvc, qkvc, proj, g, beta, grow, s0, norm_g.reshape(1, d))


def _mem_attn_kernel(q_ref, k_ref, v_ref, o_ref):
    for h in range(XA_HEADS):
        cs = slice(h * XA_HEAD_DIM, (h + 1) * XA_HEAD_DIM)
        s = _dot_nt(q_ref[:, cs].astype(BF16), k_ref[0, :, cs].astype(BF16)) * (XA_HEAD_DIM ** -0.5)
        e = jnp.exp(s - jnp.max(s, axis=-1, keepdims=True))
        p = e / jnp.sum(e, axis=-1, keepdims=True)
        o_ref[:, cs] = _dot(p.astype(BF16), v_ref[0, :, cs].astype(BF16))


def _mem_attention(proj, mem_k, mem_v, *, bsz, seq, tl):
    nl = seq // tl
    m = mem_k.shape[1]
    return pl.pallas_call(
        _mem_attn_kernel,
        grid=(bsz, nl),
        in_specs=[
            pl.BlockSpec((tl, XA_DIM), lambda b_, i: (b_ * nl + i, COL_XQ // XA_DIM)),
            pl.BlockSpec((1, m, XA_DIM), lambda b_, i: (b_, 0, 0)),
            pl.BlockSpec((1, m, XA_DIM), lambda b_, i: (b_, 0, 0)),
        ],
        out_specs=pl.BlockSpec((tl, XA_DIM), lambda b_, i: (b_ * nl + i, 0)),
        out_shape=jax.ShapeDtypeStruct((bsz * seq, XA_DIM), F32),
        compiler_params=_cparams("parallel", "arbitrary"),
        name="mem_attention",
    )(proj, mem_k, mem_v)


def _cached_attn_kernel(q_ref, k_ref, v_ref, o_ref, *, tl):
    halves = XA_HEAD_DIM // LANES
    group = halves * XA_HEADS
    rows = XA_HEADS * tl
    q2 = jnp.concatenate([q_ref[:, h * XA_HEAD_DIM + c * LANES:h * XA_HEAD_DIM + (c + 1) * LANES]
                          for c in range(halves) for h in range(XA_HEADS)], axis=0)
    s_all = _dot_nt(q2.astype(BF16), k_ref[0].astype(BF16))
    n = s_all.shape[1]
    s = s_all[:rows]
    for c in range(1, halves):
        s = s + pltpu.roll(s_all[c * rows:(c + 1) * rows], n - c * XA_HEADS, axis=1)
    col = lax.broadcasted_iota(jnp.int32, (rows, n), 1)
    row = lax.broadcasted_iota(jnp.int32, (rows, n), 0)
    assert group & (group - 1) == 0 and tl & (tl - 1) == 0
    own = (col & (group - 1)) == (row >> (tl.bit_length() - 1))
    s = jnp.where(own, s * (XA_HEAD_DIM ** -0.5), -jnp.inf)
    e = jnp.exp(s - jnp.max(s, axis=-1, keepdims=True))
    p = (e / jnp.sum(e, axis=-1, keepdims=True)).astype(BF16)
    p_all = jnp.concatenate([p] + [pltpu.roll(p.astype(F32), c * XA_HEADS, axis=1).astype(BF16)
                                   for c in range(1, halves)], axis=0)
    o = _dot(p_all, v_ref[0].astype(BF16))
    for c in range(halves):
        for h in range(XA_HEADS):
            r0 = (c * XA_HEADS + h) * tl
            o_ref[:, h * XA_HEAD_DIM + c * LANES:h * XA_HEAD_DIM + (c + 1) * LANES] = o[r0:r0 + tl]


def _cached_attention(proj, cache_k, cache_v, *, bsz, seq):
    m = cache_k.shape[1]
    halves = XA_HEAD_DIM // LANES
    rows = m * halves * XA_HEADS

    def stored_view(a):
        a = a.reshape(bsz, m, XA_HEADS, halves, LANES)
        return jnp.transpose(a, (0, 1, 3, 2, 4)).reshape(bsz, rows, LANES)

    return pl.pallas_call(
        functools.partial(_cached_attn_kernel, tl=seq),
        grid=(bsz,),
        in_specs=[
            pl.BlockSpec((seq, XA_DIM), lambda b_: (b_, COL_XQ // XA_DIM)),
            pl.BlockSpec((1, rows, LANES), lambda b_: (b_, 0, 0)),
            pl.BlockSpec((1, rows, LANES), lambda b_: (b_, 0, 0)),
        ],
        out_specs=pl.BlockSpec((seq, XA_DIM), lambda b_: (b_, 0)),
        out_shape=jax.ShapeDtypeStruct((bsz * seq, XA_DIM), F32),
        compiler_params=_cparams("parallel"),
        name="cached_attention",
    )(proj, stored_view(cache_k), stored_view(cache_v))


def _merge_kernel(c_ref, o_ref, a_ref, gc_ref, gd_ref, gx_ref, wc_ref, wd_ref, wx_ref, m_ref):
    yc = _dot(c_ref[...].astype(BF16), wc_ref[...])
    yd = _dot(o_ref[...].astype(BF16), wd_ref[...])
    yx = _dot(a_ref[...].astype(BF16), wx_ref[...])
    m_ref[...] = (jax.nn.sigmoid(gc_ref[...]) * yc + jax.nn.sigmoid(gd_ref[...]) * yd
                  + jax.nn.sigmoid(gx_ref[...]) * yx).astype(BF16)


def _merge(c, o, a, proj, wc, wd, wx, *, tm, tn):
    t = c.shape[0]
    dm = wc.shape[1]
    gcol = COL_GATES // tn
    gstep = dm // tn
    return pl.pallas_call(
        _merge_kernel,
        grid=(t // tm, dm // tn),
        in_specs=[
            pl.BlockSpec((tm, CONV_DIM), lambda i, j: (i, 0)),
            pl.BlockSpec((tm, DN_DIM), lambda i, j: (i, 0)),
            pl.BlockSpec((tm, XA_DIM), lambda i, j: (i, 0)),
            pl.BlockSpec((tm, tn), lambda i, j: (i, gcol + j)),
            pl.BlockSpec((tm, tn), lambda i, j: (i, gcol + gstep + j)),
            pl.BlockSpec((tm, tn), lambda i, j: (i, gcol + 2 * gstep + j)),
            pl.BlockSpec((CONV_DIM, tn), lambda i, j: (0, j)),
            pl.BlockSpec((DN_DIM, tn), lambda i, j: (0, j)),
            pl.BlockSpec((XA_DIM, tn), lambda i, j: (0, j)),
        ],
        out_specs=pl.BlockSpec((tm, tn), lambda i, j: (i, j)),
        out_shape=jax.ShapeDtypeStruct((t, dm), BF16),
        compiler_params=_cparams("parallel", "arbitrary"),
        name="merge",
    )(c, o, a, proj, proj, proj, wc, wd, wx)


def _out_proj_kernel(m_ref, w_ref, x_ref, o_ref):
    o_ref[...] = x_ref[...] + _dot(m_ref[...], w_ref[...])


def _out_proj(merged, w, x, *, tm, tn):
    t, dm = x.shape
    return pl.pallas_call(
        _out_proj_kernel,
        grid=(t // tm, dm // tn),
        in_specs=[
            pl.BlockSpec((tm, dm), lambda i, j: (i, 0)),
            pl.BlockSpec((dm, tn), lambda i, j: (0, j)),
            pl.BlockSpec((tm, tn), lambda i, j: (i, j)),
        ],
        out_specs=pl.BlockSpec((tm, tn), lambda i, j: (i, j)),
        out_shape=jax.ShapeDtypeStruct((t, dm), F32),
        compiler_params=_cparams("parallel", "arbitrary"),
        name="out_proj",
    )(merged, w, x)


def _top16(s, tm):
    key = lax.broadcasted_iota(jnp.int32, s.shape, 0).astype(F32)
    work = s
    rank = jnp.full(s.shape, float(PEER_TOPK), F32)
    vals = []
    for a in range(PEER_TOPK):
        m = jnp.max(work, axis=0, keepdims=True)
        idx = jnp.min(jnp.where(work == m, key, float(PEER_N_KEYS)), axis=0, keepdims=True)
        sel = key == idx
        rank = jnp.where(sel, float(a), rank)
        work = jnp.where(sel, -jnp.inf, work)
        vals.append(m)
    return vals, rank


def _peer_route_kernel(q_ref, k1_ref, k2_ref, e1_ref, e2_ref, r2_ref, nrow_ref, *, tm, heads):
    k1 = k1_ref[...].astype(BF16)
    k2 = k2_ref[...].astype(BF16)
    for hh in range(heads):
        q = q_ref[:, hh * PEER_DK:(hh + 1) * PEER_DK]
        e1_ref[hh], e2_ref[hh], r2_ref[hh], nrow_ref[hh] = _route_head(q, k1, k2, tm)


def _route_head(q, k1, k2, tm):
    half = PEER_DK // 2
    s1 = _dot_nt(k1, q[:, :half].astype(BF16))
    s2 = _dot_nt(k2, q[:, half:].astype(BF16))
    t1, rank1 = _top16(s1, tm)
    t2, rank2 = _top16(s2, tm)

    k = PEER_TOPK
    cands = [(a, b) for a in range(k) for b in range(k) if (a + 1) * (b + 1) <= k]
    no_pos = float(k * k)
    work, pos = [], []
    for g in range(0, len(cands), SUBLANES):
        group = cands[g:g + SUBLANES]
        pad = SUBLANES - len(group)
        work.append(jnp.concatenate([t1[a] + t2[b] for a, b in group]
                                    + [jnp.full((1, tm), -jnp.inf, F32)] * pad, axis=0))
        pos.append(jnp.concatenate([jnp.full((1, tm), float(a * k + b), F32) for a, b in group]
                                   + [jnp.full((1, tm), no_pos, F32)] * pad, axis=0))
    row_id = lax.broadcasted_iota(jnp.int32, (k, tm), 0).astype(F32)
    n_won = jnp.zeros((k, tm), F32)
    tops = []
    for _ in range(k):
        m = functools.reduce(jnp.maximum, work)
        m = jnp.max(m, axis=0, keepdims=True)
        p = functools.reduce(jnp.minimum, [jnp.where(w_ == m, p_, no_pos) for w_, p_ in zip(work, pos)])
        p = jnp.min(p, axis=0, keepdims=True)
        work = [jnp.where(p_ == p, -jnp.inf, w_) for w_, p_ in zip(work, pos)]
        n_won = n_won + (row_id == jnp.floor(p * (1.0 / k))).astype(F32)
        tops.append(m)
    zsum = jnp.zeros((1, tm), F32)
    for m in tops:
        zsum = zsum + jnp.exp(m - tops[0])
    nrow = jnp.zeros(s1.shape, F32)
    for a in range(k):
        nrow = jnp.where(rank1 == float(a), n_won[a:a + 1, :], nrow)

    return jnp.exp(s1 - t1[0]) / zsum, jnp.exp(s2 - t2[0]), rank2, nrow


def _peer_route(q, k1, k2, *, tm):
    t = q.shape[0]
    nk = PEER_N_KEYS
    out = jax.ShapeDtypeStruct((PEER_HEADS, nk, t), F32)
    heads = 8
    spec = pl.BlockSpec((heads, nk, tm), lambda i, h: (h, 0, i))
    return pl.pallas_call(
        functools.partial(_peer_route_kernel, tm=tm, heads=heads),
        grid=(t // tm, PEER_HEADS // heads),
        in_specs=[
            pl.BlockSpec((tm, heads * PEER_DK), lambda i, h: (i, h)),
            pl.BlockSpec((nk, PEER_DK // 2), lambda i, h: (0, 0)),
            pl.BlockSpec((nk, PEER_DK // 2), lambda i, h: (0, 0)),
        ],
        out_specs=[spec, spec, spec, spec],
        out_shape=[out, out, out, out],
        compiler_params=_cparams("parallel", "arbitrary"),
        name="peer_route",
    )(q, k1, k2)


def _gelu_tanh(x):
    c = 0.7978845608028654
    return x * (0.5 + 0.5 * jnp.tanh(x * (c + (c * 0.044715) * (x * x))))


def _peer_dense_kernel(x_ref, gn_ref, gf_ref, e1_ref, nrow_ref, e2_ref, r2_ref, u_ref, v_ref, y_ref,
                       h_ref, s_ref, p_ref, *, tm, eb, nj):
    j = pl.program_id(1)
    nk = PEER_N_KEYS

    @pl.when(j == 0)
    def _():
        h_ref[...] = _rms(x_ref[...], gn_ref[...]).astype(BF16)
        y_ref[...] = jnp.zeros_like(y_ref)

    s_ref[...] = _dot_nt(u_ref[...], h_ref[...])
    for bi in range(eb):
        for th in range(tm // LANES):
            ts = slice(th * LANES, (th + 1) * LANES)
            w = jnp.zeros((nk, LANES), F32)
            for h in range(PEER_HEADS):
                sel = r2_ref[h, :, ts] < nrow_ref[h, bi:bi + 1, ts]
                w = jnp.where(sel, w + e1_ref[h, bi:bi + 1, ts] * e2_ref[h, :, ts], w)
            act = _gelu_tanh(s_ref[bi * nk:(bi + 1) * nk, ts])
            p_ref[ts, bi * nk:(bi + 1) * nk] = (w * act).T.astype(BF16)
    y_ref[...] += _dot(p_ref[...], v_ref[...])

    @pl.when(j == nj - 1)
    def _():
        y_ref[...] = _rms(x_ref[...] + y_ref[...], gf_ref[...])


def _peer_dense(x, g_ffn, g_final, e1, e2, r2, nrow, u16, v16, *, tm, eb):
    t, dm = x.shape
    nk = PEER_N_KEYS
    ne = u16.shape[0]
    nj = ne // (eb * nk)
    return pl.pallas_call(
        functools.partial(_peer_dense_kernel, tm=tm, eb=eb, nj=nj),
        grid=(t // tm, nj),
        in_specs=[
            pl.BlockSpec((tm, dm), lambda i, j: (i, 0)),
            pl.BlockSpec((1, dm), lambda i, j: (0, 0)),
            pl.BlockSpec((1, dm), lambda i, j: (0, 0)),
            pl.BlockSpec((PEER_HEADS, eb, tm), lambda i, j: (0, j, i)),
            pl.BlockSpec((PEER_HEADS, eb, tm), lambda i, j: (0, j, i)),
            pl.BlockSpec((PEER_HEADS, nk, tm), lambda i, j: (0, 0, i)),
            pl.BlockSpec((PEER_HEADS, nk, tm), lambda i, j: (0, 0, i)),
            pl.BlockSpec((eb * nk, dm), lambda i, j: (j, 0)),
            pl.BlockSpec((eb * nk, dm), lambda i, j: (j, 0)),
        ],
        out_specs=pl.BlockSpec((tm, dm), lambda i, j: (i, 0)),
        out_shape=jax.ShapeDtypeStruct((t, dm), F32),
        scratch_shapes=[
            pltpu.VMEM((tm, dm), BF16),
            pltpu.VMEM((eb * nk, tm), F32),
            pltpu.VMEM((tm, eb * nk), BF16),
        ],
        compiler_params=_cparams("parallel", "arbitrary"),
        name="peer_dense",
    )(x, g_ffn.reshape(1, dm), g_final.reshape(1, dm), e1, nrow, e2, r2, u16, v16)


def _arrange_w_in(w_in):
    d = w_in.shape[0]
    sizes = (CONV_DIM, CONV_DIM, 3 * DN_DIM, DN_DIM, DN_HEADS, DN_HEADS, XA_DIM, 2048, 2048, 2048)
    offs = [0]
    for s in sizes:
        offs.append(offs[-1] + s)
    glu = w_in[:, offs[0]:offs[2]]
    qkv = w_in[:, offs[2]:offs[3]]
    z = w_in[:, offs[3]:offs[4]]
    ab = w_in[:, offs[4]:offs[6]]
    xq = w_in[:, offs[6]:offs[7]]
    gates = w_in[:, offs[7]:offs[10]]
    pad = jnp.zeros((d, PROJ_COLS - COL_AB - 2 * DN_HEADS), w_in.dtype)
    return jnp.concatenate([qkv, glu, z, gates, xq, ab, pad], axis=1).astype(BF16)


def _layer(x, mem_k, mem_v, conv_buf, qkv_buf, delta_state, w, *, chunk, tiles, g_final):
    bsz, seq, dm = x.shape
    t = bsz * seq
    assert seq % chunk == 0
    tiles = {name: min(size, t) for name, size in tiles.items()}
    x2d = x.reshape(t, dm)
    proj = _norm_matmul(x2d, w["norm_mix"], w["w_in"], tm=tiles["tm_proj"], tn=PROJ_TN, name="in_proj")
    c, new_conv = _conv_branch(proj, conv_buf, w["conv_dw"], w["conv_dw_b"], w["conv_ln_g"], w["conv_ln_b"],
                               bsz=bsz, seq=seq, tt=tiles["tt_conv"], bb=min(tiles["seqs_per_step"], bsz))
    qkvc, g, beta, new_qkv = _dn_prep(proj, qkv_buf, w["dn_conv_w"], w["dn_a_log"], w["dn_dt_bias"],
                                      bsz=bsz, seq=seq, tt=tiles["tt_dn"], bb=min(tiles["seqs_per_step"], bsz))
    o, new_state = _dn_chunk(qkvc, proj, g, beta, delta_state, w["dn_norm_g"], bsz=bsz, seq=seq, c=chunk)
    if mem_k.ndim == 4:
        a = _cached_attention(proj, mem_k, mem_v, bsz=bsz, seq=seq)
    else:
        a = _mem_attention(proj, mem_k, mem_v, bsz=bsz, seq=seq, tl=tiles["tl_attn"])
    merged = _merge(c, o, a, proj, w["w_conv_out"], w["w_dn_out"], w["w_xa_out"], tm=tiles["tm_merge"], tn=1024)
    x2 = _out_proj(merged, w["w_out"], x2d, tm=tiles["tm_merge"], tn=1024)
    q = _norm_matmul(x2, w["norm_ffn"], w["w_peer_q"], tm=tiles["tm_merge"], tn=1024, name="peer_q")
    e1, e2, r2, nrow = _peer_route(q, w["peer_keys_1"], w["peer_keys_2"], tm=LANES)
    y = _peer_dense(x2, w["norm_ffn"], g_final, e1, e2, r2, nrow, w["peer_u"], w["peer_v"],
                    tm=tiles["tm_peer"], eb=8)
    return y.reshape(bsz, seq, dm), new_conv, new_qkv, new_state


PROMPT_TILES = dict(tm_proj=1024, tt_conv=256, tt_dn=128, seqs_per_step=1, tl_attn=256, tm_merge=512, tm_peer=512)
SAMPLE_TILES = dict(tm_proj=1024, tt_conv=8, tt_dn=8, seqs_per_step=8, tl_attn=8, tm_merge=512, tm_peer=512)


def kernel(x_prompt, x_sample, mem_prompt, cache_mem_k, cache_mem_v, state_conv, state_qkv_conv, state_delta, norm_mix, norm_mem, w_in, conv_dw, conv_dw_b, conv_ln_g, conv_ln_b, w_conv_out, dn_conv_w, dn_a_log, dn_dt_bias, dn_norm_g, w_dn_out, w_mem_kv, w_xa_out, w_out, norm_ffn, w_peer_q, peer_keys_1, peer_keys_2, peer_u, peer_v, norm_final):
    depth = w_in.shape[0]
    assert depth == 1, "the final norm is fused into the last layer's PEER kernel; one layer is supported"
    bp, sp, dm = x_prompt.shape
    bs, ss, _ = x_sample.shape
    n_mem = mem_prompt.shape[1]
    def layer0(a):
        return a.reshape(a.shape[1:])

    w = dict(
        norm_mix=layer0(norm_mix), w_in=_arrange_w_in(layer0(w_in)), conv_dw=layer0(conv_dw),
        conv_dw_b=layer0(conv_dw_b), conv_ln_g=layer0(conv_ln_g), conv_ln_b=layer0(conv_ln_b),
        w_conv_out=layer0(w_conv_out).astype(BF16), dn_conv_w=layer0(dn_conv_w), dn_a_log=layer0(dn_a_log),
        dn_dt_bias=layer0(dn_dt_bias), dn_norm_g=layer0(dn_norm_g), w_dn_out=layer0(w_dn_out).astype(BF16),
        w_xa_out=layer0(w_xa_out).astype(BF16), w_out=layer0(w_out).astype(BF16), norm_ffn=layer0(norm_ffn),
        w_peer_q=layer0(w_peer_q).astype(BF16), peer_keys_1=layer0(peer_keys_1), peer_keys_2=layer0(peer_keys_2),
        peer_u=layer0(peer_u).astype(BF16), peer_v=layer0(peer_v).astype(BF16),
    )
    mem2d = mem_prompt.reshape(bp * n_mem, dm)
    wkv = layer0(w_mem_kv).astype(BF16)
    mk = _norm_matmul(mem2d, layer0(norm_mem), wkv[:, :XA_DIM], tm=bp * n_mem, tn=512, name="mem_k")
    mv = _norm_matmul(mem2d, layer0(norm_mem), wkv[:, XA_DIM:], tm=bp * n_mem, tn=512, name="mem_v")
    mk = mk.reshape(bp, n_mem, XA_DIM)
    mv = mv.reshape(bp, n_mem, XA_DIM)
    yp, conv_p, qkv_p, delta_p = _layer(
        x_prompt, mk, mv,
        jnp.zeros((bp, CONV_K - 1, CONV_DIM), F32),
        jnp.zeros((bp, DN_CONV_K - 1, 3 * DN_DIM), F32),
        jnp.zeros((bp, DN_HEADS, DN_HEAD_DIM, DN_HEAD_DIM), F32),
        w, chunk=min(DN_CHUNK, sp), tiles=PROMPT_TILES, g_final=norm_final)
    ys, conv_s, qkv_s, delta_s = _layer(
        x_sample, layer0(cache_mem_k), layer0(cache_mem_v),
        layer0(state_conv), layer0(state_qkv_conv), layer0(state_delta),
        w, chunk=min(DN_CHUNK, ss), tiles=SAMPLE_TILES, g_final=norm_final)
    kv_shape = (1, bp, n_mem, XA_HEADS, XA_HEAD_DIM)
    return (yp, ys, mk.reshape(kv_shape), mv.reshape(kv_shape), conv_p[None], qkv_p[None], delta_p[None],
            conv_s[None], qkv_s[None], delta_s[None])
```

```python
import functools

import jax
import jax.numpy as jnp
from jax import lax
from jax.experimental import pallas as pl
from jax.experimental.pallas import tpu as pltpu

F32 = jnp.float32
BF16 = jnp.bfloat16

NORM_EPS = 1e-6
LANES = 128
SUBLANES = 8
VMEM_LIMIT_BYTES = 56 * 1024 * 1024

CONV_DIM = 1024
CONV_K = 31
DN_HEADS = 16
DN_HEAD_DIM = 128
DN_DIM = DN_HEADS * DN_HEAD_DIM
DN_CONV_K = 4
DN_CHUNK = 64
XA_HEADS = 4
XA_HEAD_DIM = 256
XA_DIM = XA_HEADS * XA_HEAD_DIM
PEER_HEADS = 8
PEER_N_KEYS = 128
PEER_DK = 256
PEER_TOPK = 16

COL_QKV = 0
COL_GLU = 3 * DN_DIM
COL_Z = COL_GLU + 2 * CONV_DIM
COL_GATES = COL_Z + DN_DIM
COL_XQ = COL_GATES + 3 * 2048
COL_AB = COL_XQ + XA_DIM
PROJ_TN = 1280
PROJ_COLS = 14 * PROJ_TN


def _cparams(*dims):
    return pltpu.CompilerParams(dimension_semantics=dims, vmem_limit_bytes=VMEM_LIMIT_BYTES)


def _dot(a, b, precision=None):
    return jnp.dot(a, b, preferred_element_type=F32, precision=precision)


def _dot_nt(a, b):
    return lax.dot_general(a, b, (((1,), (1,)), ((), ())), preferred_element_type=F32)


def _dot_tn(a, b):
    return lax.dot_general(a, b, (((0,), (0,)), ((), ())), preferred_element_type=F32)


def _rms(x, g):
    return x * lax.rsqrt(jnp.mean(x * x, axis=-1, keepdims=True) + NORM_EPS) * g


def _silu(x):
    return x * jax.nn.sigmoid(x)


def _norm_matmul_kernel(x_ref, g_ref, w_ref, o_ref, h_ref):
    @pl.when(pl.program_id(1) == 0)
    def _():
        h_ref[...] = _rms(x_ref[...], g_ref[...]).astype(BF16)

    o_ref[...] = _dot(h_ref[...], w_ref[...])


def _norm_matmul(x, g, w, *, tm, tn, name):
    t, d = x.shape
    n = w.shape[1]
    return pl.pallas_call(
        _norm_matmul_kernel,
        grid=(t // tm, n // tn),
        in_specs=[
            pl.BlockSpec((tm, d), lambda i, j: (i, 0)),
            pl.BlockSpec((1, d), lambda i, j: (0, 0)),
            pl.BlockSpec((d, tn), lambda i, j: (0, j)),
        ],
        out_specs=pl.BlockSpec((tm, tn), lambda i, j: (i, j)),
        out_shape=jax.ShapeDtypeStruct((t, n), F32),
        scratch_shapes=[pltpu.VMEM((tm, d), BF16)],
        compiler_params=_cparams("parallel", "arbitrary"),
        name=name,
    )(x, g.reshape(1, d), w)


def _conv_kernel(a_ref, g_ref, buf_ref, w_ref, b_ref, lng_ref, lnb_ref, c_ref, nbuf_ref, xp_ref, sh_ref, y_ref,
                 *, tt, nt, bb):
    i = pl.program_id(1)
    head = 32
    off = head - (CONV_K - 1)

    for bi in range(bb):
        rows_b = slice(bi * tt, (bi + 1) * tt)

        def load_history(bi=bi):
            xp_ref[0:off, :] = jnp.zeros((off, CONV_DIM), F32)
            xp_ref[off:head, :] = buf_ref[bi]

        if nt == 1:
            load_history()
        else:
            pl.when(i == 0)(load_history)

        xp_ref[head:head + tt, :] = a_ref[rows_b, :] * jax.nn.sigmoid(g_ref[rows_b, :])

        span = tt + head - SUBLANES
        for s in range(1, SUBLANES):
            sh_ref[s - 1] = xp_ref[pl.ds(s, span), :]

        rt = min(32, tt)
        cw = 512
        for r in range(tt // rt):
            for cb in range(CONV_DIM // cw):
                cs = slice(cb * cw, (cb + 1) * cw)
                acc = jnp.zeros((rt, cw), F32)
                for j in range(CONV_K):
                    s = (off + j) % SUBLANES
                    rows = pl.ds(r * rt + off + j - s, rt)
                    tap = xp_ref[rows, cs] if s == 0 else sh_ref[s - 1, rows, cs]
                    acc = acc + w_ref[j:j + 1, cs] * tap
                y_ref[r * rt:(r + 1) * rt, cs] = acc + b_ref[:, cs]

        y = y_ref[...]
        mu = jnp.mean(y, axis=-1, keepdims=True)
        yc = y - mu
        var = jnp.mean(yc * yc, axis=-1, keepdims=True)
        c_ref[rows_b, :] = _silu(yc * lax.rsqrt(var + NORM_EPS) * lng_ref[...] + lnb_ref[...])

        tail = xp_ref[tt:tt + head, :]
        if nt == 1:
            nbuf_ref[bi] = tail[off:, :]
        else:
            def store_history(tail=tail, bi=bi):
                nbuf_ref[bi] = tail[off:, :]

            pl.when(i == nt - 1)(store_history)
            xp_ref[0:head, :] = tail


def _conv_branch(proj, buf, w, b, lng, lnb, *, bsz, seq, tt, bb):
    nt = seq // tt
    assert bb == 1 or nt == 1
    ca = COL_GLU // CONV_DIM
    row = lambda b_, i: b_ * nt + i
    vec = lambda v: v.reshape(1, CONV_DIM)
    return pl.pallas_call(
        functools.partial(_conv_kernel, tt=tt, nt=nt, bb=bb),
        grid=(bsz // bb, nt),
        in_specs=[
            pl.BlockSpec((bb * tt, CONV_DIM), lambda b_, i: (row(b_, i), ca)),
            pl.BlockSpec((bb * tt, CONV_DIM), lambda b_, i: (row(b_, i), ca + 1)),
            pl.BlockSpec((bb, CONV_K - 1, CONV_DIM), lambda b_, i: (b_, 0, 0)),
            pl.BlockSpec((CONV_K, CONV_DIM), lambda b_, i: (0, 0)),
            pl.BlockSpec((1, CONV_DIM), lambda b_, i: (0, 0)),
            pl.BlockSpec((1, CONV_DIM), lambda b_, i: (0, 0)),
            pl.BlockSpec((1, CONV_DIM), lambda b_, i: (0, 0)),
        ],
        out_specs=[
            pl.BlockSpec((bb * tt, CONV_DIM), lambda b_, i: (row(b_, i), 0)),
            pl.BlockSpec((bb, CONV_K - 1, CONV_DIM), lambda b_, i: (b_, 0, 0)),
        ],
        out_shape=[
            jax.ShapeDtypeStruct((bsz * seq, CONV_DIM), F32),
            jax.ShapeDtypeStruct((bsz, CONV_K - 1, CONV_DIM), F32),
        ],
        scratch_shapes=[
            pltpu.VMEM((tt + 32, CONV_DIM), F32),
            pltpu.VMEM((SUBLANES - 1, tt + 32 - SUBLANES, CONV_DIM), F32),
            pltpu.VMEM((tt, CONV_DIM), F32),
        ],
        compiler_params=_cparams("parallel", "arbitrary"),
        name="conv_branch",
    )(proj, proj, buf, w, vec(b), vec(lng), vec(lnb))


def _dn_prep_kernel(x_ref, ab_ref, buf_ref, w_ref, alog_ref, dtb_ref, o_ref, g_ref, beta_ref, nbuf_ref, xp_ref,
                    *, tt, nt, bb):
    i = pl.program_id(1)
    head = SUBLANES
    off = head - (DN_CONV_K - 1)

    for bi in range(bb):
        rows_b = slice(bi * tt, (bi + 1) * tt)

        def load_history(bi=bi):
            xp_ref[0:off, :] = jnp.zeros((off, 3 * DN_DIM), F32)
            xp_ref[off:head, :] = buf_ref[bi]

        if nt == 1:
            load_history()
        else:
            pl.when(i == 0)(load_history)

        xp_ref[head:head + tt, :] = x_ref[rows_b, :]

        for blk in range(3 * DN_HEADS):
            cs = slice(blk * DN_HEAD_DIM, (blk + 1) * DN_HEAD_DIM)
            acc = jnp.zeros((tt, DN_HEAD_DIM), F32)
            for j in range(DN_CONV_K):
                acc = acc + w_ref[j:j + 1, cs] * xp_ref[pl.ds(off + j, tt), cs]
            y = _silu(acc)
            if blk < 2 * DN_HEADS:
                y = y * lax.rsqrt(jnp.sum(y * y, axis=-1, keepdims=True) + NORM_EPS)
                if blk < DN_HEADS:
                    y = y * (DN_HEAD_DIM ** -0.5)
            o_ref[rows_b, cs] = y

        tail = xp_ref[tt:tt + head, :]
        if nt == 1:
            nbuf_ref[bi] = tail[off:, :]
        else:
            def store_history(tail=tail, bi=bi):
                nbuf_ref[bi] = tail[off:, :]

            pl.when(i == nt - 1)(store_history)
            xp_ref[0:head, :] = tail

    ab = ab_ref[...]
    a_raw = ab[:, 0:DN_HEADS]
    b_raw = ab[:, DN_HEADS:2 * DN_HEADS]
    sp_in = a_raw + dtb_ref[...]
    softplus = jnp.maximum(sp_in, 0.0) + jnp.log1p(jnp.exp(-jnp.abs(sp_in)))
    g_ref[...] = -jnp.exp(alog_ref[...]) * softplus
    beta_ref[...] = jax.nn.sigmoid(b_raw)


def _dn_prep(proj, buf, w, a_log, dt_bias, *, bsz, seq, tt, bb):
    nt = seq // tt
    assert bb == 1 or nt == 1
    dq = 3 * DN_DIM
    row = lambda b_, i: b_ * nt + i
    return pl.pallas_call(
        functools.partial(_dn_prep_kernel, tt=tt, nt=nt, bb=bb),
        grid=(bsz // bb, nt),
        in_specs=[
            pl.BlockSpec((bb * tt, dq), lambda b_, i: (row(b_, i), COL_QKV // dq)),
            pl.BlockSpec((bb * tt, LANES), lambda b_, i: (row(b_, i), COL_AB // LANES)),
            pl.BlockSpec((bb, DN_CONV_K - 1, dq), lambda b_, i: (b_, 0, 0)),
            pl.BlockSpec((DN_CONV_K, dq), lambda b_, i: (0, 0)),
            pl.BlockSpec((1, DN_HEADS), lambda b_, i: (0, 0)),
            pl.BlockSpec((1, DN_HEADS), lambda b_, i: (0, 0)),
        ],
        out_specs=[
            pl.BlockSpec((bb * tt, dq), lambda b_, i: (row(b_, i), 0)),
            pl.BlockSpec((bb * tt, DN_HEADS), lambda b_, i: (row(b_, i), 0)),
            pl.BlockSpec((bb * tt, DN_HEADS), lambda b_, i: (row(b_, i), 0)),
            pl.BlockSpec((bb, DN_CONV_K - 1, dq), lambda b_, i: (b_, 0, 0)),
        ],
        out_shape=[
            jax.ShapeDtypeStruct((bsz * seq, dq), F32),
            jax.ShapeDtypeStruct((bsz * seq, DN_HEADS), F32),
            jax.ShapeDtypeStruct((bsz * seq, DN_HEADS), F32),
            jax.ShapeDtypeStruct((bsz, DN_CONV_K - 1, dq), F32),
        ],
        scratch_shapes=[pltpu.VMEM((tt + SUBLANES, dq), F32)],
        compiler_params=_cparams("parallel", "arbitrary"),
        name="dn_prep",
    )(proj, proj, buf, w, a_log.reshape(1, DN_HEADS), dt_bias.reshape(1, DN_HEADS))


def _run_interleaved(gens):
    while gens:
        alive = []
        for g in gens:
            try:
                next(g)
                alive.append(g)
            except StopIteration:
                pass
        gens = alive


def _unit_lower_inverse(a, masks):
    eye, diag_block, off_blocks = masks
    ad = jnp.where(diag_block, a, 0.0)
    def mm(p, q):
        return _dot(p.astype(BF16), q.astype(BF16))

    a2 = mm(ad, ad)
    yield
    a4 = mm(a2, a2)
    x = mm(eye - ad, eye + a2)
    yield
    x = mm(x, eye + a4)
    yield
    for off_block in off_blocks:
        y = mm(x, jnp.where(off_block, a, 0.0))
        yield
        x = x - mm(y, x)
        yield
    return x


def _inverse_masks(c):
    ri = lax.broadcasted_iota(jnp.int32, (c, c), 0)
    ci = lax.broadcasted_iota(jnp.int32, (c, c), 1)
    eye = (ri == ci).astype(F32)
    diag_block = (ri >> 3) == (ci >> 3)
    off_blocks = []
    shift = 3
    while (1 << shift) < c:
        off_blocks.append(((ri >> (shift + 1)) == (ci >> (shift + 1))) & ((ri >> shift) != (ci >> shift)))
        shift += 1
    return eye, diag_block, off_blocks


def _dn_head(q, k, v, z, gcol, beta, grow, s, ng, tri, inv_masks, c):
    incl, strict, upper = tri
    gc_col = jnp.sum(jnp.where(incl, grow, 0.0), axis=1, keepdims=True)
    gc_row = jnp.sum(jnp.where(upper, gcol, 0.0), axis=0, keepdims=True)
    decay_incl = jnp.exp(jnp.where(incl, gc_col - gc_row, -jnp.inf))
    decay_strict = jnp.where(strict, decay_incl, 0.0)

    kb = k * beta
    kq = _dot_nt(jnp.concatenate([kb, q], axis=0).astype(BF16), k.astype(BF16))
    yield
    t_inv = yield from _unit_lower_inverse(kq[:c] * decay_strict, inv_masks)
    e_gc = jnp.exp(gc_col)
    rhs = jnp.concatenate([v * beta, kb * e_gc], axis=1)
    sol = _dot(t_inv.astype(BF16), rhs.astype(BF16))
    yield
    value = sol[:, :DN_HEAD_DIM]
    k_cum = sol[:, DN_HEAD_DIM:]
    qk = kq[c:] * decay_incl
    gc_last = gc_col[c - 1:c, :]
    k_dec = k * jnp.exp(gc_last - gc_col)

    ks = _dot(jnp.concatenate([k_cum, q * e_gc], axis=0).astype(BF16), s.astype(BF16))
    yield
    u16 = (value - ks[:c]).astype(BF16)
    o = ks[c:] + _dot(qk.astype(BF16), u16)
    s_new = s * jnp.exp(gc_last) + _dot_tn(k_dec.astype(BF16), u16)
    return _rms(o, ng) * _silu(z), s_new


def _dn_chunk_kernel(q_ref, k_ref, v_ref, z_ref, g_ref, beta_ref, grow_ref, s0_ref, ng_ref, o_ref, s_ref, *, c, bb):
    @pl.when(pl.program_id(1) == 0)
    def _():
        s_ref[...] = s0_ref[...]

    ri = lax.broadcasted_iota(jnp.int32, (c, c), 0)
    ci = lax.broadcasted_iota(jnp.int32, (c, c), 1)
    tri = (ri >= ci, ri > ci, ri <= ci)
    inv_masks = _inverse_masks(c)
    ng = ng_ref[...]
    d = DN_HEAD_DIM

    def head(bi, h):
        cs = slice(h * d, (h + 1) * d)
        rs = slice(bi * c, (bi + 1) * c)
        o, s_new = yield from _dn_head(q_ref[rs, cs], k_ref[rs, cs], v_ref[rs, cs], z_ref[rs, cs], g_ref[rs, h:h + 1],
                                       beta_ref[rs, h:h + 1], grow_ref[bi, 0, h:h + 1, :], s_ref[bi, h], ng, tri,
                                       inv_masks, c)
        o_ref[rs, cs] = o
        s_ref[bi, h] = s_new

    _run_interleaved([head(bi, h) for bi in range(bb) for h in range(DN_HEADS)])


def _dn_chunk(qkvc, proj, g, beta, s0, norm_g, *, bsz, seq, c, bb):
    n = seq // c
    assert bb == 1 or n == 1
    h_ = DN_HEADS
    d = DN_HEAD_DIM
    grow = jnp.transpose(g.reshape(bsz, n, c, h_), (0, 1, 3, 2))
    row = lambda b_, n_: (b_ * n + n_)
    return pl.pallas_call(
        functools.partial(_dn_chunk_kernel, c=c, bb=bb),
        grid=(bsz // bb, n),
        in_specs=[
            pl.BlockSpec((bb * c, DN_DIM), lambda b_, n_: (row(b_, n_), 0)),
            pl.BlockSpec((bb * c, DN_DIM), lambda b_, n_: (row(b_, n_), 1)),
            pl.BlockSpec((bb * c, DN_DIM), lambda b_, n_: (row(b_, n_), 2)),
            pl.BlockSpec((bb * c, DN_DIM), lambda b_, n_: (row(b_, n_), COL_Z // DN_DIM)),
            pl.BlockSpec((bb * c, h_), lambda b_, n_: (row(b_, n_), 0)),
            pl.BlockSpec((bb * c, h_), lambda b_, n_: (row(b_, n_), 0)),
            pl.BlockSpec((bb, 1, h_, c), lambda b_, n_: (b_, n_, 0, 0)),
            pl.BlockSpec((bb, h_, d, d), lambda b_, n_: (b_, 0, 0, 0)),
            pl.BlockSpec((1, d), lambda b_, n_: (0, 0)),
        ],
        out_specs=[
            pl.BlockSpec((bb * c, DN_DIM), lambda b_, n_: (row(b_, n_), 0)),
            pl.BlockSpec((bb, h_, d, d), lambda b_, n_: (b_, 0, 0, 0)),
        ],
        out_shape=[
            jax.ShapeDtypeStruct((bsz * seq, DN_DIM), F32),
            jax.ShapeDtypeStruct((bsz, h_, d, d), F32),
        ],
        compiler_params=_cparams("parallel", "arbitrary"),
        name="dn_chunk",
    )(qkvc, qkvc, qkvc, proj, g, beta, grow, s0, norm_g.reshape(1, d))


def _mem_attn_kernel(q_ref, k_ref, v_ref, o_ref):
    for h in range(XA_HEADS):
        cs = slice(h * XA_HEAD_DIM, (h + 1) * XA_HEAD_DIM)
        s = _dot_nt(q_ref[:, cs].astype(BF16), k_ref[0, :, cs].astype(BF16)) * (XA_HEAD_DIM ** -0.5)
        e = jnp.exp(s - jnp.max(s, axis=-1, keepdims=True))
        p = e / jnp.sum(e, axis=-1, keepdims=True)
        o_ref[:, cs] = _dot(p.astype(BF16), v_ref[0, :, cs].astype(BF16))


def _mem_attention(proj, mem_k, mem_v, *, bsz, seq, tl):
    nl = seq // tl
    m = mem_k.shape[1]
    return pl.pallas_call(
        _mem_attn_kernel,
        grid=(bsz, nl),
        in_specs=[
            pl.BlockSpec((tl, XA_DIM), lambda b_, i: (b_ * nl + i, COL_XQ // XA_DIM)),
            pl.BlockSpec((1, m, XA_DIM), lambda b_, i: (b_, 0, 0)),
            pl.BlockSpec((1, m, XA_DIM), lambda b_, i: (b_, 0, 0)),
        ],
        out_specs=pl.BlockSpec((tl, XA_DIM), lambda b_, i: (b_ * nl + i, 0)),
        out_shape=jax.ShapeDtypeStruct((bsz * seq, XA_DIM), F32),
        compiler_params=_cparams("parallel", "arbitrary"),
        name="mem_attention",
    )(proj, mem_k, mem_v)


def _cached_attn_kernel(q_ref, k_ref, v_ref, o_ref, *, tl):
    halves = XA_HEAD_DIM // LANES
    group = halves * XA_HEADS
    rows = XA_HEADS * tl
    q2 = jnp.concatenate([q_ref[:, h * XA_HEAD_DIM + c * LANES:h * XA_HEAD_DIM + (c + 1) * LANES]
                          for c in range(halves) for h in range(XA_HEADS)], axis=0)
    s_all = _dot_nt(q2.astype(BF16), k_ref[0].astype(BF16))
    n = s_all.shape[1]
    s = s_all[:rows]
    for c in range(1, halves):
        s = s + pltpu.roll(s_all[c * rows:(c + 1) * rows], n - c * XA_HEADS, axis=1)
    col = lax.broadcasted_iota(jnp.int32, (rows, n), 1)
    row = lax.broadcasted_iota(jnp.int32, (rows, n), 0)
    assert group & (group - 1) == 0 and tl & (tl - 1) == 0
    own = (col & (group - 1)) == (row >> (tl.bit_length() - 1))
    s = jnp.where(own, s * (XA_HEAD_DIM ** -0.5), -jnp.inf)
    e = jnp.exp(s - jnp.max(s, axis=-1, keepdims=True))
    p = (e / jnp.sum(e, axis=-1, keepdims=True)).astype(BF16)
    p_all = jnp.concatenate([p] + [pltpu.roll(p.astype(F32), c * XA_HEADS, axis=1).astype(BF16)
                                   for c in range(1, halves)], axis=0)
    o = _dot(p_all, v_ref[0].astype(BF16))
    for c in range(halves):
        for h in range(XA_HEADS):
            r0 = (c * XA_HEADS + h) * tl
            o_ref[:, h * XA_HEAD_DIM + c * LANES:h * XA_HEAD_DIM + (c + 1) * LANES] = o[r0:r0 + tl]


def _cached_attention(proj, cache_k, cache_v, *, bsz, seq):
    m = cache_k.shape[1]
    halves = XA_HEAD_DIM // LANES
    rows = m * halves * XA_HEADS

    def stored_view(a):
        a = a.reshape(bsz, m, XA_HEADS, halves, LANES)
        return jnp.transpose(a, (0, 1, 3, 2, 4)).reshape(bsz, rows, LANES)

    return pl.pallas_call(
        functools.partial(_cached_attn_kernel, tl=seq),
        grid=(bsz,),
        in_specs=[
            pl.BlockSpec((seq, XA_DIM), lambda b_: (b_, COL_XQ // XA_DIM)),
            pl.BlockSpec((1, rows, LANES), lambda b_: (b_, 0, 0)),
            pl.BlockSpec((1, rows, LANES), lambda b_: (b_, 0, 0)),
        ],
        out_specs=pl.BlockSpec((seq, XA_DIM), lambda b_: (b_, 0)),
        out_shape=jax.ShapeDtypeStruct((bsz * seq, XA_DIM), F32),
        compiler_params=_cparams("parallel"),
        name="cached_attention",
    )(proj, stored_view(cache_k), stored_view(cache_v))


def _merge_kernel(c_ref, o_ref, a_ref, gc_ref, gd_ref, gx_ref, wc_ref, wd_ref, wx_ref, m_ref):
    yc = _dot(c_ref[...].astype(BF16), wc_ref[...])
    yd = _dot(o_ref[...].astype(BF16), wd_ref[...])
    yx = _dot(a_ref[...].astype(BF16), wx_ref[...])
    m_ref[...] = (jax.nn.sigmoid(gc_ref[...]) * yc + jax.nn.sigmoid(gd_ref[...]) * yd
                  + jax.nn.sigmoid(gx_ref[...]) * yx).astype(BF16)


def _merge(c, o, a, proj, wc, wd, wx, *, tm, tn):
    t = c.shape[0]
    dm = wc.shape[1]
    gcol = COL_GATES // tn
    gstep = dm // tn
    return pl.pallas_call(
        _merge_kernel,
        grid=(t // tm, dm // tn),
        in_specs=[
            pl.BlockSpec((tm, CONV_DIM), lambda i, j: (i, 0)),
            pl.BlockSpec((tm, DN_DIM), lambda i, j: (i, 0)),
            pl.BlockSpec((tm, XA_DIM), lambda i, j: (i, 0)),
            pl.BlockSpec((tm, tn), lambda i, j: (i, gcol + j)),
            pl.BlockSpec((tm, tn), lambda i, j: (i, gcol + gstep + j)),
            pl.BlockSpec((tm, tn), lambda i, j: (i, gcol + 2 * gstep + j)),
            pl.BlockSpec((CONV_DIM, tn), lambda i, j: (0, j)),
            pl.BlockSpec((DN_DIM, tn), lambda i, j: (0, j)),
            pl.BlockSpec((XA_DIM, tn), lambda i, j: (0, j)),
        ],
        out_specs=pl.BlockSpec((tm, tn), lambda i, j: (i, j)),
        out_shape=jax.ShapeDtypeStruct((t, dm), BF16),
        compiler_params=_cparams("parallel", "arbitrary"),
        name="merge",
    )(c, o, a, proj, proj, proj, wc, wd, wx)


def _out_proj_kernel(m_ref, w_ref, x_ref, o_ref):
    o_ref[...] = x_ref[...] + _dot(m_ref[...], w_ref[...])


def _out_proj(merged, w, x, *, tm, tn):
    t, dm = x.shape
    return pl.pallas_call(
        _out_proj_kernel,
        grid=(t // tm, dm // tn),
        in_specs=[
            pl.BlockSpec((tm, dm), lambda i, j: (i, 0)),
            pl.BlockSpec((dm, tn), lambda i, j: (0, j)),
            pl.BlockSpec((tm, tn), lambda i, j: (i, j)),
        ],
        out_specs=pl.BlockSpec((tm, tn), lambda i, j: (i, j)),
        out_shape=jax.ShapeDtypeStruct((t, dm), F32),
        compiler_params=_cparams("parallel", "arbitrary"),
        name="out_proj",
    )(merged, w, x)


def _top16(s, tm):
    key = lax.broadcasted_iota(jnp.int32, s.shape, 0).astype(F32)
    work = s
    rank = jnp.full(s.shape, float(PEER_TOPK), F32)
    vals = []
    for a in range(PEER_TOPK):
        m = jnp.max(work, axis=0, keepdims=True)
        idx = jnp.min(jnp.where(work == m, key, float(PEER_N_KEYS)), axis=0, keepdims=True)
        sel = key == idx
        rank = jnp.where(sel, float(a), rank)
        work = jnp.where(sel, -jnp.inf, work)
        vals.append(m)
    return vals, rank


def _peer_route_kernel(q_ref, k1_ref, k2_ref, e1_ref, e2_ref, r2_ref, nrow_ref, *, tm, heads):
    k1 = k1_ref[...].astype(BF16)
    k2 = k2_ref[...].astype(BF16)
    for hh in range(heads):
        q = q_ref[:, hh * PEER_DK:(hh + 1) * PEER_DK]
        e1_ref[hh], e2_ref[hh], r2_ref[hh], nrow_ref[hh] = _route_head(q, k1, k2, tm)


def _route_head(q, k1, k2, tm):
    half = PEER_DK // 2
    s1 = _dot_nt(k1, q[:, :half].astype(BF16))
    s2 = _dot_nt(k2, q[:, half:].astype(BF16))
    t1, rank1 = _top16(s1, tm)
    t2, rank2 = _top16(s2, tm)

    k = PEER_TOPK
    cands = [(a, b) for a in range(k) for b in range(k) if (a + 1) * (b + 1) <= k]
    no_pos = float(k * k)
    work, pos = [], []
    for g in range(0, len(cands), SUBLANES):
        group = cands[g:g + SUBLANES]
        pad = SUBLANES - len(group)
        work.append(jnp.concatenate([t1[a] + t2[b] for a, b in group]
                                    + [jnp.full((1, tm), -jnp.inf, F32)] * pad, axis=0))
        pos.append(jnp.concatenate([jnp.full((1, tm), float(a * k + b), F32) for a, b in group]
                                   + [jnp.full((1, tm), no_pos, F32)] * pad, axis=0))
    row_id = lax.broadcasted_iota(jnp.int32, (k, tm), 0).astype(F32)
    n_won = jnp.zeros((k, tm), F32)
    tops = []
    for _ in range(k):
        m = functools.reduce(jnp.maximum, work)
        m = jnp.max(m, axis=0, keepdims=True)
        p = functools.reduce(jnp.minimum, [jnp.where(w_ == m, p_, no_pos) for w_, p_ in zip(work, pos)])
        p = jnp.min(p, axis=0, keepdims=True)
        work = [jnp.where(p_ == p, -jnp.inf, w_) for w_, p_ in zip(work, pos)]
        n_won = n_won + (row_id == jnp.floor(p * (1.0 / k))).astype(F32)
        tops.append(m)
    zsum = jnp.zeros((1, tm), F32)
    for m in tops:
        zsum = zsum + jnp.exp(m - tops[0])
    nrow = jnp.zeros(s1.shape, F32)
    for a in range(k):
        nrow = jnp.where(rank1 == float(a), n_won[a:a + 1, :], nrow)

    return jnp.exp(s1 - t1[0]) / zsum, jnp.exp(s2 - t2[0]), rank2, nrow


def _peer_route(q, k1, k2, *, tm):
    t = q.shape[0]
    nk = PEER_N_KEYS
    out = jax.ShapeDtypeStruct((PEER_HEADS, nk, t), F32)
    heads = 8
    spec = pl.BlockSpec((heads, nk, tm), lambda i, h: (h, 0, i))
    return pl.pallas_call(
        functools.partial(_peer_route_kernel, tm=tm, heads=heads),
        grid=(t // tm, PEER_HEADS // heads),
        in_specs=[
            pl.BlockSpec((tm, heads * PEER_DK), lambda i, h: (i, h)),
            pl.BlockSpec((nk, PEER_DK // 2), lambda i, h: (0, 0)),
            pl.BlockSpec((nk, PEER_DK // 2), lambda i, h: (0, 0)),
        ],
        out_specs=[spec, spec, spec, spec],
        out_shape=[out, out, out, out],
        compiler_params=_cparams("parallel", "arbitrary"),
        name="peer_route",
    )(q, k1, k2)


def _gelu_tanh(x):
    c = 0.7978845608028654
    return x * (0.5 + 0.5 * jnp.tanh(x * (c + (c * 0.044715) * (x * x))))


def _peer_dense_kernel(x_ref, gn_ref, gf_ref, e1_ref, nrow_ref, e2_ref, r2_ref, u_ref, v_ref, y_ref,
                       h_ref, s_ref, p_ref, *, tm, eb, nj):
    j = pl.program_id(1)
    nk = PEER_N_KEYS

    @pl.when(j == 0)
    def _():
        h_ref[...] = _rms(x_ref[...], gn_ref[...]).astype(BF16)
        y_ref[...] = jnp.zeros_like(y_ref)

    s_ref[...] = _dot_nt(u_ref[...], h_ref[...])
    for bi in range(eb):
        for th in range(tm // LANES):
            ts = slice(th * LANES, (th + 1) * LANES)
            w = jnp.zeros((nk, LANES), F32)
            for h in range(PEER_HEADS):
                sel = r2_ref[h, :, ts] < nrow_ref[h, bi:bi + 1, ts]
                w = jnp.where(sel, w + e1_ref[h, bi:bi + 1, ts] * e2_ref[h, :, ts], w)
            act = _gelu_tanh(s_ref[bi * nk:(bi + 1) * nk, ts])
            p_ref[ts, bi * nk:(bi + 1) * nk] = (w * act).T.astype(BF16)
    y_ref[...] += _dot(p_ref[...], v_ref[...])

    @pl.when(j == nj - 1)
    def _():
        y_ref[...] = _rms(x_ref[...] + y_ref[...], gf_ref[...])


def _peer_dense(x, g_ffn, g_final, e1, e2, r2, nrow, u16, v16, *, tm, eb):
    t, dm = x.shape
    nk = PEER_N_KEYS
    ne = u16.shape[0]
    nj = ne // (eb * nk)
    return pl.pallas_call(
        functools.partial(_peer_dense_kernel, tm=tm, eb=eb, nj=nj),
        grid=(t // tm, nj),
        in_specs=[
            pl.BlockSpec((tm, dm), lambda i, j: (i, 0)),
            pl.BlockSpec((1, dm), lambda i, j: (0, 0)),
            pl.BlockSpec((1, dm), lambda i, j: (0, 0)),
            pl.BlockSpec((PEER_HEADS, eb, tm), lambda i, j: (0, j, i)),
            pl.BlockSpec((PEER_HEADS, eb, tm), lambda i, j: (0, j, i)),
            pl.BlockSpec((PEER_HEADS, nk, tm), lambda i, j: (0, 0, i)),
            pl.BlockSpec((PEER_HEADS, nk, tm), lambda i, j: (0, 0, i)),
            pl.BlockSpec((eb * nk, dm), lambda i, j: (j, 0)),
            pl.BlockSpec((eb * nk, dm), lambda i, j: (j, 0)),
        ],
        out_specs=pl.BlockSpec((tm, dm), lambda i, j: (i, 0)),
        out_shape=jax.ShapeDtypeStruct((t, dm), F32),
        scratch_shapes=[
            pltpu.VMEM((tm, dm), BF16),
            pltpu.VMEM((eb * nk, tm), F32),
            pltpu.VMEM((tm, eb * nk), BF16),
        ],
        compiler_params=_cparams("parallel", "arbitrary"),
        name="peer_dense",
    )(x, g_ffn.reshape(1, dm), g_final.reshape(1, dm), e1, nrow, e2, r2, u16, v16)


def _arrange_kernel(w_ref, o_ref, *, pieces, used):
    for src, size, dst in pieces:
        o_ref[:, dst:dst + size] = w_ref[:, src:src + size].astype(BF16)
    o_ref[:, used:] = jnp.zeros((o_ref.shape[0], o_ref.shape[1] - used), BF16)


def _arrange_w_in(w_in):
    d, n_in = w_in.shape
    sizes = (CONV_DIM, CONV_DIM, 3 * DN_DIM, DN_DIM, DN_HEADS, DN_HEADS, XA_DIM, 2048, 2048, 2048)
    offs = [0]
    for s in sizes:
        offs.append(offs[-1] + s)
    pieces = ((offs[0], offs[2] - offs[0], COL_GLU), (offs[2], offs[3] - offs[2], COL_QKV),
              (offs[3], offs[4] - offs[3], COL_Z), (offs[4], offs[6] - offs[4], COL_AB),
              (offs[6], offs[7] - offs[6], COL_XQ), (offs[7], offs[10] - offs[7], COL_GATES))
    rows = LANES
    return pl.pallas_call(
        functools.partial(_arrange_kernel, pieces=pieces, used=COL_AB + 2 * DN_HEADS),
        grid=(d // rows,),
        in_specs=[pl.BlockSpec((rows, n_in), lambda i: (i, 0))],
        out_specs=pl.BlockSpec((rows, PROJ_COLS), lambda i: (i, 0)),
        out_shape=jax.ShapeDtypeStruct((d, PROJ_COLS), BF16),
        compiler_params=_cparams("parallel"),
        name="arrange_w_in",
    )(w_in)


def _layer(x, mem_k, mem_v, conv_buf, qkv_buf, delta_state, w, *, chunk, tiles, g_final):
    bsz, seq, dm = x.shape
    t = bsz * seq
    assert seq % chunk == 0
    tiles = {name: min(size, t) for name, size in tiles.items()}
    x2d = x.reshape(t, dm)
    proj = _norm_matmul(x2d, w["norm_mix"], w["w_in"], tm=tiles["tm_proj"], tn=PROJ_TN, name="in_proj")
    c, new_conv = _conv_branch(proj, conv_buf, w["conv_dw"], w["conv_dw_b"], w["conv_ln_g"], w["conv_ln_b"],
                               bsz=bsz, seq=seq, tt=tiles["tt_conv"], bb=min(tiles["seqs_per_step"], bsz))
    qkvc, g, beta, new_qkv = _dn_prep(proj, qkv_buf, w["dn_conv_w"], w["dn_a_log"], w["dn_dt_bias"],
                                      bsz=bsz, seq=seq, tt=tiles["tt_dn"], bb=min(tiles["seqs_per_step"], bsz))
    o, new_state = _dn_chunk(qkvc, proj, g, beta, delta_state, w["dn_norm_g"], bsz=bsz, seq=seq, c=chunk,
                             bb=2 if seq == chunk and bsz % 2 == 0 else 1)
    if mem_k.ndim == 4:
        a = _cached_attention(proj, mem_k, mem_v, bsz=bsz, seq=seq)
    else:
        a = _mem_attention(proj, mem_k, mem_v, bsz=bsz, seq=seq, tl=tiles["tl_attn"])
    merged = _merge(c, o, a, proj, w["w_conv_out"], w["w_dn_out"], w["w_xa_out"], tm=tiles["tm_merge"], tn=1024)
    x2 = _out_proj(merged, w["w_out"], x2d, tm=tiles["tm_merge"], tn=1024)
    q = _norm_matmul(x2, w["norm_ffn"], w["w_peer_q"], tm=tiles["tm_merge"], tn=1024, name="peer_q")
    e1, e2, r2, nrow = _peer_route(q, w["peer_keys_1"], w["peer_keys_2"], tm=LANES)
    y = _peer_dense(x2, w["norm_ffn"], g_final, e1, e2, r2, nrow, w["peer_u"], w["peer_v"],
                    tm=tiles["tm_peer"], eb=8)
    return y.reshape(bsz, seq, dm), new_conv, new_qkv, new_state


PROMPT_TILES = dict(tm_proj=1024, tt_conv=256, tt_dn=128, seqs_per_step=1, tl_attn=256, tm_merge=512, tm_peer=512)
SAMPLE_TILES = dict(tm_proj=1024, tt_conv=8, tt_dn=8, seqs_per_step=8, tl_attn=8, tm_merge=512, tm_peer=512)


def kernel(x_prompt, x_sample, mem_prompt, cache_mem_k, cache_mem_v, state_conv, state_qkv_conv, state_delta, norm_mix, norm_mem, w_in, conv_dw, conv_dw_b, conv_ln_g, conv_ln_b, w_conv_out, dn_conv_w, dn_a_log, dn_dt_bias, dn_norm_g, w_dn_out, w_mem_kv, w_xa_out, w_out, norm_ffn, w_peer_q, peer_keys_1, peer_keys_2, peer_u, peer_v, norm_final):
    depth = w_in.shape[0]
    assert depth == 1, "the final norm is fused into the last layer's PEER kernel; one layer is supported"
    bp, sp, dm = x_prompt.shape
    bs, ss, _ = x_sample.shape
    n_mem = mem_prompt.shape[1]
    def layer0(a):
        return a.reshape(a.shape[1:])

    w = dict(
        norm_mix=layer0(norm_mix), w_in=_arrange_w_in(layer0(w_in)), conv_dw=layer0(conv_dw),
        conv_dw_b=layer0(conv_dw_b), conv_ln_g=layer0(conv_ln_g), conv_ln_b=layer0(conv_ln_b),
        w_conv_out=layer0(w_conv_out).astype(BF16), dn_conv_w=layer0(dn_conv_w), dn_a_log=layer0(dn_a_log),
        dn_dt_bias=layer0(dn_dt_bias), dn_norm_g=layer0(dn_norm_g), w_dn_out=layer0(w_dn_out).astype(BF16),
        w_xa_out=layer0(w_xa_out).astype(BF16), w_out=layer0(w_out).astype(BF16), norm_ffn=layer0(norm_ffn),
        w_peer_q=layer0(w_peer_q).astype(BF16), peer_keys_1=layer0(peer_keys_1), peer_keys_2=layer0(peer_keys_2),
        peer_u=layer0(peer_u).astype(BF16), peer_v=layer0(peer_v).astype(BF16),
    )
    mem2d = mem_prompt.reshape(bp * n_mem, dm)
    wkv = layer0(w_mem_kv).astype(BF16)
    mk = _norm_matmul(mem2d, layer0(norm_mem), wkv[:, :XA_DIM], tm=bp * n_mem, tn=512, name="mem_k")
    mv = _norm_matmul(mem2d, layer0(norm_mem), wkv[:, XA_DIM:], tm=bp * n_mem, tn=512, name="mem_v")
    mk = mk.reshape(bp, n_mem, XA_DIM)
    mv = mv.reshape(bp, n_mem, XA_DIM)
    yp, conv_p, qkv_p, delta_p = _layer(
        x_prompt, mk, mv,
        jnp.zeros((bp, CONV_K - 1, CONV_DIM), F32),
        jnp.zeros((bp, DN_CONV_K - 1, 3 * DN_DIM), F32),
        jnp.zeros((bp, DN_HEADS, DN_HEAD_DIM, DN_HEAD_DIM), F32),
        w, chunk=min(DN_CHUNK, sp), tiles=PROMPT_TILES, g_final=norm_final)
    ys, conv_s, qkv_s, delta_s = _layer(
        x_sample, layer0(cache_mem_k), layer0(cache_mem_v),
        layer0(state_conv), layer0(state_qkv_conv), layer0(state_delta),
        w, chunk=min(DN_CHUNK, ss), tiles=SAMPLE_TILES, g_final=norm_final)
    kv_shape = (1, bp, n_mem, XA_HEADS, XA_HEAD_DIM)
    return (yp, ys, mk.reshape(kv_shape), mv.reshape(kv_shape), conv_p[None], qkv_p[None], delta_p[None],
            conv_s[None], qkv_s[None], delta_s[None])
```

```python
import functools

import jax
import jax.numpy as jnp
from jax import lax
from jax.experimental import pallas as pl
from jax.experimental.pallas import tpu as pltpu

F32 = jnp.float32
BF16 = jnp.bfloat16

NORM_EPS = 1e-6
LANES = 128
SUBLANES = 8
VMEM_LIMIT_BYTES = 56 * 1024 * 1024

CONV_DIM = 1024
CONV_K = 31
DN_HEADS = 16
DN_HEAD_DIM = 128
DN_DIM = DN_HEADS * DN_HEAD_DIM
DN_CONV_K = 4
DN_CHUNK = 64
XA_HEADS = 4
XA_HEAD_DIM = 256
XA_DIM = XA_HEADS * XA_HEAD_DIM
PEER_HEADS = 8
PEER_N_KEYS = 128
PEER_DK = 256
PEER_TOPK = 16

COL_QKV = 0
COL_GLU = 3 * DN_DIM
COL_Z = COL_GLU + 2 * CONV_DIM
COL_GATES = COL_Z + DN_DIM
COL_XQ = COL_GATES + 3 * 2048
COL_AB = COL_XQ + XA_DIM
PROJ_TN = 1280
PROJ_COLS = 14 * PROJ_TN


def _cparams(*dims):
    return pltpu.CompilerParams(dimension_semantics=dims, vmem_limit_bytes=VMEM_LIMIT_BYTES)


def _dot(a, b, precision=None):
    return jnp.dot(a, b, preferred_element_type=F32, precision=precision)


def _dot_nt(a, b):
    return lax.dot_general(a, b, (((1,), (1,)), ((), ())), preferred_element_type=F32)


def _dot_tn(a, b):
    return lax.dot_general(a, b, (((0,), (0,)), ((), ())), preferred_element_type=F32)


def _rms(x, g):
    return x * lax.rsqrt(jnp.mean(x * x, axis=-1, keepdims=True) + NORM_EPS) * g


def _silu(x):
    return x * jax.nn.sigmoid(x)


def _norm_matmul_kernel(x_ref, g_ref, w_ref, o_ref, h_ref, *, w_is_transposed):
    @pl.when(pl.program_id(1) == 0)
    def _():
        h_ref[...] = _rms(x_ref[...], g_ref[...]).astype(BF16)

    o_ref[...] = _dot_nt(h_ref[...], w_ref[...]) if w_is_transposed else _dot(h_ref[...], w_ref[...])


def _norm_matmul(x, g, w, *, tm, tn, name, w_is_transposed=False):
    t, d = x.shape
    n = w.shape[0] if w_is_transposed else w.shape[1]
    w_spec = pl.BlockSpec((tn, d), lambda i, j: (j, 0)) if w_is_transposed else pl.BlockSpec((d, tn), lambda i, j: (0, j))
    return pl.pallas_call(
        functools.partial(_norm_matmul_kernel, w_is_transposed=w_is_transposed),
        grid=(t // tm, n // tn),
        in_specs=[
            pl.BlockSpec((tm, d), lambda i, j: (i, 0)),
            pl.BlockSpec((1, d), lambda i, j: (0, 0)),
            w_spec,
        ],
        out_specs=pl.BlockSpec((tm, tn), lambda i, j: (i, j)),
        out_shape=jax.ShapeDtypeStruct((t, n), F32),
        scratch_shapes=[pltpu.VMEM((tm, d), BF16)],
        compiler_params=_cparams("parallel", "arbitrary"),
        name=name,
    )(x, g.reshape(1, d), w)


def _conv_kernel(a_ref, g_ref, buf_ref, w_ref, b_ref, lng_ref, lnb_ref, c_ref, nbuf_ref, xp_ref, sh_ref, y_ref,
                 *, tt, nt, bb):
    i = pl.program_id(1)
    head = 32
    off = head - (CONV_K - 1)

    for bi in range(bb):
        rows_b = slice(bi * tt, (bi + 1) * tt)

        def load_history(bi=bi):
            xp_ref[0:off, :] = jnp.zeros((off, CONV_DIM), F32)
            xp_ref[off:head, :] = buf_ref[bi]

        if nt == 1:
            load_history()
        else:
            pl.when(i == 0)(load_history)

        xp_ref[head:head + tt, :] = a_ref[rows_b, :] * jax.nn.sigmoid(g_ref[rows_b, :])

        span = tt + head - SUBLANES
        for s in range(1, SUBLANES):
            sh_ref[s - 1] = xp_ref[pl.ds(s, span), :]

        rt = min(32, tt)
        cw = 512
        for r in range(tt // rt):
            for cb in range(CONV_DIM // cw):
                cs = slice(cb * cw, (cb + 1) * cw)
                acc = jnp.zeros((rt, cw), F32)
                for j in range(CONV_K):
                    s = (off + j) % SUBLANES
                    rows = pl.ds(r * rt + off + j - s, rt)
                    tap = xp_ref[rows, cs] if s == 0 else sh_ref[s - 1, rows, cs]
                    acc = acc + w_ref[j:j + 1, cs] * tap
                y_ref[r * rt:(r + 1) * rt, cs] = acc + b_ref[:, cs]

        y = y_ref[...]
        mu = jnp.mean(y, axis=-1, keepdims=True)
        yc = y - mu
        var = jnp.mean(yc * yc, axis=-1, keepdims=True)
        c_ref[rows_b, :] = _silu(yc * lax.rsqrt(var + NORM_EPS) * lng_ref[...] + lnb_ref[...])

        tail = xp_ref[tt:tt + head, :]
        if nt == 1:
            nbuf_ref[bi] = tail[off:, :]
        else:
            def store_history(tail=tail, bi=bi):
                nbuf_ref[bi] = tail[off:, :]

            pl.when(i == nt - 1)(store_history)
            xp_ref[0:head, :] = tail


def _conv_branch(proj, buf, w, b, lng, lnb, *, bsz, seq, tt, bb):
    nt = seq // tt
    assert bb == 1 or nt == 1
    ca = COL_GLU // CONV_DIM
    row = lambda b_, i: b_ * nt + i
    vec = lambda v: v.reshape(1, CONV_DIM)
    return pl.pallas_call(
        functools.partial(_conv_kernel, tt=tt, nt=nt, bb=bb),
        grid=(bsz // bb, nt),
        in_specs=[
            pl.BlockSpec((bb * tt, CONV_DIM), lambda b_, i: (row(b_, i), ca)),
            pl.BlockSpec((bb * tt, CONV_DIM), lambda b_, i: (row(b_, i), ca + 1)),
            pl.BlockSpec((bb, CONV_K - 1, CONV_DIM), lambda b_, i: (b_, 0, 0)),
            pl.BlockSpec((CONV_K, CONV_DIM), lambda b_, i: (0, 0)),
            pl.BlockSpec((1, CONV_DIM), lambda b_, i: (0, 0)),
            pl.BlockSpec((1, CONV_DIM), lambda b_, i: (0, 0)),
            pl.BlockSpec((1, CONV_DIM), lambda b_, i: (0, 0)),
        ],
        out_specs=[
            pl.BlockSpec((bb * tt, CONV_DIM), lambda b_, i: (row(b_, i), 0)),
            pl.BlockSpec((bb, CONV_K - 1, CONV_DIM), lambda b_, i: (b_, 0, 0)),
        ],
        out_shape=[
            jax.ShapeDtypeStruct((bsz * seq, CONV_DIM), F32),
            jax.ShapeDtypeStruct((bsz, CONV_K - 1, CONV_DIM), F32),
        ],
        scratch_shapes=[
            pltpu.VMEM((tt + 32, CONV_DIM), F32),
            pltpu.VMEM((SUBLANES - 1, tt + 32 - SUBLANES, CONV_DIM), F32),
            pltpu.VMEM((tt, CONV_DIM), F32),
        ],
        compiler_params=_cparams("parallel", "arbitrary"),
        name="conv_branch",
    )(proj, proj, buf, w, vec(b), vec(lng), vec(lnb))


def _dn_prep_kernel(x_ref, ab_ref, buf_ref, w_ref, alog_ref, dtb_ref, o_ref, g_ref, beta_ref, nbuf_ref, xp_ref,
                    *, tt, nt, bb):
    i = pl.program_id(1)
    head = SUBLANES
    off = head - (DN_CONV_K - 1)

    for bi in range(bb):
        rows_b = slice(bi * tt, (bi + 1) * tt)

        def load_history(bi=bi):
            xp_ref[0:off, :] = jnp.zeros((off, 3 * DN_DIM), F32)
            xp_ref[off:head, :] = buf_ref[bi]

        if nt == 1:
            load_history()
        else:
            pl.when(i == 0)(load_history)

        xp_ref[head:head + tt, :] = x_ref[rows_b, :]

        for blk in range(3 * DN_HEADS):
            cs = slice(blk * DN_HEAD_DIM, (blk + 1) * DN_HEAD_DIM)
            acc = jnp.zeros((tt, DN_HEAD_DIM), F32)
            for j in range(DN_CONV_K):
                acc = acc + w_ref[j:j + 1, cs] * xp_ref[pl.ds(off + j, tt), cs]
            y = _silu(acc)
            if blk < 2 * DN_HEADS:
                y = y * lax.rsqrt(jnp.sum(y * y, axis=-1, keepdims=True) + NORM_EPS)
                if blk < DN_HEADS:
                    y = y * (DN_HEAD_DIM ** -0.5)
            o_ref[rows_b, cs] = y

        tail = xp_ref[tt:tt + head, :]
        if nt == 1:
            nbuf_ref[bi] = tail[off:, :]
        else:
            def store_history(tail=tail, bi=bi):
                nbuf_ref[bi] = tail[off:, :]

            pl.when(i == nt - 1)(store_history)
            xp_ref[0:head, :] = tail

    ab = ab_ref[...]
    a_raw = ab[:, 0:DN_HEADS]
    b_raw = ab[:, DN_HEADS:2 * DN_HEADS]
    sp_in = a_raw + dtb_ref[...]
    softplus = jnp.maximum(sp_in, 0.0) + jnp.log1p(jnp.exp(-jnp.abs(sp_in)))
    g_ref[...] = -jnp.exp(alog_ref[...]) * softplus
    beta_ref[...] = jax.nn.sigmoid(b_raw)


def _dn_prep(proj, buf, w, a_log, dt_bias, *, bsz, seq, tt, bb):
    nt = seq // tt
    assert bb == 1 or nt == 1
    dq = 3 * DN_DIM
    row = lambda b_, i: b_ * nt + i
    return pl.pallas_call(
        functools.partial(_dn_prep_kernel, tt=tt, nt=nt, bb=bb),
        grid=(bsz // bb, nt),
        in_specs=[
            pl.BlockSpec((bb * tt, dq), lambda b_, i: (row(b_, i), COL_QKV // dq)),
            pl.BlockSpec((bb * tt, LANES), lambda b_, i: (row(b_, i), COL_AB // LANES)),
            pl.BlockSpec((bb, DN_CONV_K - 1, dq), lambda b_, i: (b_, 0, 0)),
            pl.BlockSpec((DN_CONV_K, dq), lambda b_, i: (0, 0)),
            pl.BlockSpec((1, DN_HEADS), lambda b_, i: (0, 0)),
            pl.BlockSpec((1, DN_HEADS), lambda b_, i: (0, 0)),
        ],
        out_specs=[
            pl.BlockSpec((bb * tt, dq), lambda b_, i: (row(b_, i), 0)),
            pl.BlockSpec((bb * tt, DN_HEADS), lambda b_, i: (row(b_, i), 0)),
            pl.BlockSpec((bb * tt, DN_HEADS), lambda b_, i: (row(b_, i), 0)),
            pl.BlockSpec((bb, DN_CONV_K - 1, dq), lambda b_, i: (b_, 0, 0)),
        ],
        out_shape=[
            jax.ShapeDtypeStruct((bsz * seq, dq), F32),
            jax.ShapeDtypeStruct((bsz * seq, DN_HEADS), F32),
            jax.ShapeDtypeStruct((bsz * seq, DN_HEADS), F32),
            jax.ShapeDtypeStruct((bsz, DN_CONV_K - 1, dq), F32),
        ],
        scratch_shapes=[pltpu.VMEM((tt + SUBLANES, dq), F32)],
        compiler_params=_cparams("parallel", "arbitrary"),
        name="dn_prep",
    )(proj, proj, buf, w, a_log.reshape(1, DN_HEADS), dt_bias.reshape(1, DN_HEADS))


def _run_interleaved(gens):
    while gens:
        alive = []
        for g in gens:
            try:
                next(g)
                alive.append(g)
            except StopIteration:
                pass
        gens = alive


def _unit_lower_inverse(a, masks):
    eye, diag_block, off_blocks = masks
    ad = jnp.where(diag_block, a, 0.0)
    def mm(p, q):
        return _dot(p.astype(BF16), q.astype(BF16))

    a2 = mm(ad, ad)
    yield
    a4 = mm(a2, a2)
    x = mm(eye - ad, eye + a2)
    yield
    x = mm(x, eye + a4)
    yield
    for off_block in off_blocks:
        y = mm(x, jnp.where(off_block, a, 0.0))
        yield
        x = x - mm(y, x)
        yield
    return x


def _inverse_masks(c):
    ri = lax.broadcasted_iota(jnp.int32, (c, c), 0)
    ci = lax.broadcasted_iota(jnp.int32, (c, c), 1)
    eye = (ri == ci).astype(F32)
    diag_block = (ri >> 3) == (ci >> 3)
    off_blocks = []
    shift = 3
    while (1 << shift) < c:
        off_blocks.append(((ri >> (shift + 1)) == (ci >> (shift + 1))) & ((ri >> shift) != (ci >> shift)))
        shift += 1
    return eye, diag_block, off_blocks


def _dn_head(q, k, v, z, gcol, beta, grow, s, ng, tri, inv_masks, c):
    incl, strict, upper = tri
    gc_col = jnp.sum(jnp.where(incl, grow, 0.0), axis=1, keepdims=True)
    gc_row = jnp.sum(jnp.where(upper, gcol, 0.0), axis=0, keepdims=True)
    decay_incl = jnp.exp(jnp.where(incl, gc_col - gc_row, -jnp.inf))
    decay_strict = jnp.where(strict, decay_incl, 0.0)

    kb = k * beta
    kq = _dot_nt(jnp.concatenate([kb, q], axis=0).astype(BF16), k.astype(BF16))
    yield
    t_inv = yield from _unit_lower_inverse(kq[:c] * decay_strict, inv_masks)
    e_gc = jnp.exp(gc_col)
    rhs = jnp.concatenate([v * beta, kb * e_gc], axis=1)
    sol = _dot(t_inv.astype(BF16), rhs.astype(BF16))
    yield
    value = sol[:, :DN_HEAD_DIM]
    k_cum = sol[:, DN_HEAD_DIM:]
    qk = kq[c:] * decay_incl
    gc_last = gc_col[c - 1:c, :]
    k_dec = k * jnp.exp(gc_last - gc_col)

    ks = _dot(jnp.concatenate([k_cum, q * e_gc], axis=0).astype(BF16), s.astype(BF16))
    yield
    u16 = (value - ks[:c]).astype(BF16)
    o = ks[c:] + _dot(qk.astype(BF16), u16)
    s_new = s * jnp.exp(gc_last) + _dot_tn(k_dec.astype(BF16), u16)
    return _rms(o, ng) * _silu(z), s_new


def _dn_chunk_kernel(q_ref, k_ref, v_ref, z_ref, g_ref, beta_ref, grow_ref, s0_ref, ng_ref, o_ref, s_ref, *, c, bb):
    @pl.when(pl.program_id(1) == 0)
    def _():
        s_ref[...] = s0_ref[...]

    ri = lax.broadcasted_iota(jnp.int32, (c, c), 0)
    ci = lax.broadcasted_iota(jnp.int32, (c, c), 1)
    tri = (ri >= ci, ri > ci, ri <= ci)
    inv_masks = _inverse_masks(c)
    ng = ng_ref[...]
    d = DN_HEAD_DIM

    def head(bi, h):
        cs = slice(h * d, (h + 1) * d)
        rs = slice(bi * c, (bi + 1) * c)
        o, s_new = yield from _dn_head(q_ref[rs, cs], k_ref[rs, cs], v_ref[rs, cs], z_ref[rs, cs], g_ref[rs, h:h + 1],
                                       beta_ref[rs, h:h + 1], grow_ref[bi, 0, h:h + 1, :], s_ref[bi, h], ng, tri,
                                       inv_masks, c)
        o_ref[rs, cs] = o
        s_ref[bi, h] = s_new

    _run_interleaved([head(bi, h) for bi in range(bb) for h in range(DN_HEADS)])


def _dn_chunk(qkvc, proj, g, beta, s0, norm_g, *, bsz, seq, c, bb):
    n = seq // c
    assert bb == 1 or n == 1
    h_ = DN_HEADS
    d = DN_HEAD_DIM
    grow = jnp.transpose(g.reshape(bsz, n, c, h_), (0, 1, 3, 2))
    row = lambda b_, n_: (b_ * n + n_)
    return pl.pallas_call(
        functools.partial(_dn_chunk_kernel, c=c, bb=bb),
        grid=(bsz // bb, n),
        in_specs=[
            pl.BlockSpec((bb * c, DN_DIM), lambda b_, n_: (row(b_, n_), 0)),
            pl.BlockSpec((bb * c, DN_DIM), lambda b_, n_: (row(b_, n_), 1)),
            pl.BlockSpec((bb * c, DN_DIM), lambda b_, n_: (row(b_, n_), 2)),
            pl.BlockSpec((bb * c, DN_DIM), lambda b_, n_: (row(b_, n_), COL_Z // DN_DIM)),
            pl.BlockSpec((bb * c, h_), lambda b_, n_: (row(b_, n_), 0)),
            pl.BlockSpec((bb * c, h_), lambda b_, n_: (row(b_, n_), 0)),
            pl.BlockSpec((bb, 1, h_, c), lambda b_, n_: (b_, n_, 0, 0)),
            pl.BlockSpec((bb, h_, d, d), lambda b_, n_: (b_, 0, 0, 0)),
            pl.BlockSpec((1, d), lambda b_, n_: (0, 0)),
        ],
        out_specs=[
            pl.BlockSpec((bb * c, DN_DIM), lambda b_, n_: (row(b_, n_), 0)),
            pl.BlockSpec((bb, h_, d, d), lambda b_, n_: (b_, 0, 0, 0)),
        ],
        out_shape=[
            jax.ShapeDtypeStruct((bsz * seq, DN_DIM), F32),
            jax.ShapeDtypeStruct((bsz, h_, d, d), F32),
        ],
        compiler_params=_cparams("parallel", "arbitrary"),
        name="dn_chunk",
    )(qkvc, qkvc, qkvc, proj, g, beta, grow, s0, norm_g.reshape(1, d))


def _mem_attn_kernel(q_ref, k_ref, v_ref, o_ref):
    for h in range(XA_HEADS):
        cs = slice(h * XA_HEAD_DIM, (h + 1) * XA_HEAD_DIM)
        s = _dot_nt(q_ref[:, cs].astype(BF16), k_ref[0, :, cs].astype(BF16)) * (XA_HEAD_DIM ** -0.5)
        e = jnp.exp(s - jnp.max(s, axis=-1, keepdims=True))
        p = e / jnp.sum(e, axis=-1, keepdims=True)
        o_ref[:, cs] = _dot(p.astype(BF16), v_ref[0, :, cs].astype(BF16))


def _mem_attention(proj, mem_k, mem_v, *, bsz, seq, tl):
    nl = seq // tl
    m = mem_k.shape[1]
    return pl.pallas_call(
        _mem_attn_kernel,
        grid=(bsz, nl),
        in_specs=[
            pl.BlockSpec((tl, XA_DIM), lambda b_, i: (b_ * nl + i, COL_XQ // XA_DIM)),
            pl.BlockSpec((1, m, XA_DIM), lambda b_, i: (b_, 0, 0)),
            pl.BlockSpec((1, m, XA_DIM), lambda b_, i: (b_, 0, 0)),
        ],
        out_specs=pl.BlockSpec((tl, XA_DIM), lambda b_, i: (b_ * nl + i, 0)),
        out_shape=jax.ShapeDtypeStruct((bsz * seq, XA_DIM), F32),
        compiler_params=_cparams("parallel", "arbitrary"),
        name="mem_attention",
    )(proj, mem_k, mem_v)


def _cached_attn_kernel(q_ref, k_ref, v_ref, o_ref, *, tl, bb):
    halves = XA_HEAD_DIM // LANES
    group = halves * XA_HEADS
    rows = XA_HEADS * tl
    n = k_ref.shape[1]
    col = lax.broadcasted_iota(jnp.int32, (rows, n), 1)
    row = lax.broadcasted_iota(jnp.int32, (rows, n), 0)
    assert group & (group - 1) == 0 and tl & (tl - 1) == 0
    own = (col & (group - 1)) == (row >> (tl.bit_length() - 1))
    for bi in range(bb):
        tok = slice(bi * tl, (bi + 1) * tl)
        q2 = jnp.concatenate([q_ref[tok, h * XA_HEAD_DIM + c * LANES:h * XA_HEAD_DIM + (c + 1) * LANES]
                              for c in range(halves) for h in range(XA_HEADS)], axis=0)
        s_all = _dot_nt(q2.astype(BF16), k_ref[bi].astype(BF16))
        s = s_all[:rows]
        for c in range(1, halves):
            s = s + pltpu.roll(s_all[c * rows:(c + 1) * rows], n - c * XA_HEADS, axis=1)
        s = jnp.where(own, s * (XA_HEAD_DIM ** -0.5), -jnp.inf)
        e = jnp.exp(s - jnp.max(s, axis=-1, keepdims=True))
        p = (e / jnp.sum(e, axis=-1, keepdims=True)).astype(BF16)
        p_all = jnp.concatenate([p] + [pltpu.roll(p.astype(F32), c * XA_HEADS, axis=1).astype(BF16)
                                       for c in range(1, halves)], axis=0)
        o = _dot(p_all, v_ref[bi].astype(BF16))
        for c in range(halves):
            for h in range(XA_HEADS):
                r0 = (c * XA_HEADS + h) * tl
                o_ref[tok, h * XA_HEAD_DIM + c * LANES:h * XA_HEAD_DIM + (c + 1) * LANES] = o[r0:r0 + tl]


def _cached_attention(proj, cache_k, cache_v, *, bsz, seq, bb):
    m = cache_k.shape[1]
    halves = XA_HEAD_DIM // LANES
    rows = m * halves * XA_HEADS

    def stored_view(a):
        a = a.reshape(bsz, m, XA_HEADS, halves, LANES)
        return jnp.transpose(a, (0, 1, 3, 2, 4)).reshape(bsz, rows, LANES)

    return pl.pallas_call(
        functools.partial(_cached_attn_kernel, tl=seq, bb=bb),
        grid=(bsz // bb,),
        in_specs=[
            pl.BlockSpec((bb * seq, XA_DIM), lambda b_: (b_, COL_XQ // XA_DIM)),
            pl.BlockSpec((bb, rows, LANES), lambda b_: (b_, 0, 0)),
            pl.BlockSpec((bb, rows, LANES), lambda b_: (b_, 0, 0)),
        ],
        out_specs=pl.BlockSpec((bb * seq, XA_DIM), lambda b_: (b_, 0)),
        out_shape=jax.ShapeDtypeStruct((bsz * seq, XA_DIM), F32),
        compiler_params=_cparams("parallel"),
        name="cached_attention",
    )(proj, stored_view(cache_k), stored_view(cache_v))


def _merge_kernel(c_ref, o_ref, a_ref, gc_ref, gd_ref, gx_ref, wc_ref, wd_ref, wx_ref, m_ref):
    yc = _dot(c_ref[...].astype(BF16), wc_ref[...])
    yd = _dot(o_ref[...].astype(BF16), wd_ref[...])
    yx = _dot(a_ref[...].astype(BF16), wx_ref[...])
    m_ref[...] = (jax.nn.sigmoid(gc_ref[...]) * yc + jax.nn.sigmoid(gd_ref[...]) * yd
                  + jax.nn.sigmoid(gx_ref[...]) * yx).astype(BF16)


def _merge(c, o, a, proj, wc, wd, wx, *, tm, tn):
    t = c.shape[0]
    dm = wc.shape[1]
    gcol = COL_GATES // tn
    gstep = dm // tn
    return pl.pallas_call(
        _merge_kernel,
        grid=(t // tm, dm // tn),
        in_specs=[
            pl.BlockSpec((tm, CONV_DIM), lambda i, j: (i, 0)),
            pl.BlockSpec((tm, DN_DIM), lambda i, j: (i, 0)),
            pl.BlockSpec((tm, XA_DIM), lambda i, j: (i, 0)),
            pl.BlockSpec((tm, tn), lambda i, j: (i, gcol + j)),
            pl.BlockSpec((tm, tn), lambda i, j: (i, gcol + gstep + j)),
            pl.BlockSpec((tm, tn), lambda i, j: (i, gcol + 2 * gstep + j)),
            pl.BlockSpec((CONV_DIM, tn), lambda i, j: (0, j)),
            pl.BlockSpec((DN_DIM, tn), lambda i, j: (0, j)),
            pl.BlockSpec((XA_DIM, tn), lambda i, j: (0, j)),
        ],
        out_specs=pl.BlockSpec((tm, tn), lambda i, j: (i, j)),
        out_shape=jax.ShapeDtypeStruct((t, dm), BF16),
        compiler_params=_cparams("parallel", "arbitrary"),
        name="merge",
    )(c, o, a, proj, proj, proj, wc, wd, wx)


def _out_proj_kernel(m_ref, w_ref, x_ref, o_ref):
    o_ref[...] = x_ref[...] + _dot(m_ref[...], w_ref[...])


def _out_proj(merged, w, x, *, tm, tn):
    t, dm = x.shape
    return pl.pallas_call(
        _out_proj_kernel,
        grid=(t // tm, dm // tn),
        in_specs=[
            pl.BlockSpec((tm, dm), lambda i, j: (i, 0)),
            pl.BlockSpec((dm, tn), lambda i, j: (0, j)),
            pl.BlockSpec((tm, tn), lambda i, j: (i, j)),
        ],
        out_specs=pl.BlockSpec((tm, tn), lambda i, j: (i, j)),
        out_shape=jax.ShapeDtypeStruct((t, dm), F32),
        compiler_params=_cparams("parallel", "arbitrary"),
        name="out_proj",
    )(merged, w, x)


def _top16(s, tm):
    key = lax.broadcasted_iota(jnp.int32, s.shape, 0).astype(F32)
    work = s
    rank = jnp.full(s.shape, float(PEER_TOPK), F32)
    vals = []
    for a in range(PEER_TOPK):
        m = jnp.max(work, axis=0, keepdims=True)
        idx = jnp.min(jnp.where(work == m, key, float(PEER_N_KEYS)), axis=0, keepdims=True)
        sel = key == idx
        rank = jnp.where(sel, float(a), rank)
        work = jnp.where(sel, -jnp.inf, work)
        vals.append(m)
    return vals, rank


def _peer_route_kernel(q_ref, k1_ref, k2_ref, e1_ref, e2_ref, r2_ref, nrow_ref, *, tm, heads):
    k1 = k1_ref[...].astype(BF16)
    k2 = k2_ref[...].astype(BF16)
    for hh in range(heads):
        q = q_ref[:, hh * PEER_DK:(hh + 1) * PEER_DK]
        e1_ref[hh], e2_ref[hh], r2_ref[hh], nrow_ref[hh] = _route_head(q, k1, k2, tm)


def _route_head(q, k1, k2, tm):
    half = PEER_DK // 2
    s1 = _dot_nt(k1, q[:, :half].astype(BF16))
    s2 = _dot_nt(k2, q[:, half:].astype(BF16))
    t1, rank1 = _top16(s1, tm)
    t2, rank2 = _top16(s2, tm)

    k = PEER_TOPK
    cands = [(a, b) for a in range(k) for b in range(k) if (a + 1) * (b + 1) <= k]
    no_pos = float(k * k)
    work, pos = [], []
    for g in range(0, len(cands), SUBLANES):
        group = cands[g:g + SUBLANES]
        pad = SUBLANES - len(group)
        work.append(jnp.concatenate([t1[a] + t2[b] for a, b in group]
                                    + [jnp.full((1, tm), -jnp.inf, F32)] * pad, axis=0))
        pos.append(jnp.concatenate([jnp.full((1, tm), float(a * k + b), F32) for a, b in group]
                                   + [jnp.full((1, tm), no_pos, F32)] * pad, axis=0))
    row_id = lax.broadcasted_iota(jnp.int32, (k, tm), 0).astype(F32)
    n_won = jnp.zeros((k, tm), F32)
    tops = []
    for _ in range(k):
        m = functools.reduce(jnp.maximum, work)
        m = jnp.max(m, axis=0, keepdims=True)
        p = functools.reduce(jnp.minimum, [jnp.where(w_ == m, p_, no_pos) for w_, p_ in zip(work, pos)])
        p = jnp.min(p, axis=0, keepdims=True)
        work = [jnp.where(p_ == p, -jnp.inf, w_) for w_, p_ in zip(work, pos)]
        n_won = n_won + (row_id == jnp.floor(p * (1.0 / k))).astype(F32)
        tops.append(m)
    zsum = jnp.zeros((1, tm), F32)
    for m in tops:
        zsum = zsum + jnp.exp(m - tops[0])
    nrow = jnp.zeros(s1.shape, F32)
    for a in range(k):
        nrow = jnp.where(rank1 == float(a), n_won[a:a + 1, :], nrow)

    return jnp.exp(s1 - t1[0]) / zsum, jnp.exp(s2 - t2[0]), rank2, nrow


def _peer_route(q, k1, k2, *, tm):
    t = q.shape[0]
    nk = PEER_N_KEYS
    out = jax.ShapeDtypeStruct((PEER_HEADS, nk, t), F32)
    heads = 8
    spec = pl.BlockSpec((heads, nk, tm), lambda i, h: (h, 0, i))
    return pl.pallas_call(
        functools.partial(_peer_route_kernel, tm=tm, heads=heads),
        grid=(t // tm, PEER_HEADS // heads),
        in_specs=[
            pl.BlockSpec((tm, heads * PEER_DK), lambda i, h: (i, h)),
            pl.BlockSpec((nk, PEER_DK // 2), lambda i, h: (0, 0)),
            pl.BlockSpec((nk, PEER_DK // 2), lambda i, h: (0, 0)),
        ],
        out_specs=[spec, spec, spec, spec],
        out_shape=[out, out, out, out],
        compiler_params=_cparams("parallel", "arbitrary"),
        name="peer_route",
    )(q, k1, k2)


def _gelu_tanh(x):
    c = 0.7978845608028654
    return x * (0.5 + 0.5 * jnp.tanh(x * (c + (c * 0.044715) * (x * x))))


def _peer_dense_kernel(x_ref, gn_ref, gf_ref, e1_ref, nrow_ref, e2_ref, r2_ref, u_ref, v_ref, y_ref,
                       h_ref, s_ref, p_ref, *, tm, eb, nj):
    j = pl.program_id(1)
    nk = PEER_N_KEYS

    @pl.when(j == 0)
    def _():
        h_ref[...] = _rms(x_ref[...], gn_ref[...]).astype(BF16)
        y_ref[...] = jnp.zeros_like(y_ref)

    s_ref[...] = _dot_nt(u_ref[...], h_ref[...])
    for bi in range(eb):
        for th in range(tm // LANES):
            ts = slice(th * LANES, (th + 1) * LANES)
            w = jnp.zeros((nk, LANES), F32)
            for h in range(PEER_HEADS):
                sel = r2_ref[h, :, ts] < nrow_ref[h, bi:bi + 1, ts]
                w = jnp.where(sel, w + e1_ref[h, bi:bi + 1, ts] * e2_ref[h, :, ts], w)
            act = _gelu_tanh(s_ref[bi * nk:(bi + 1) * nk, ts])
            p_ref[ts, bi * nk:(bi + 1) * nk] = (w * act).T.astype(BF16)
    y_ref[...] += _dot(p_ref[...], v_ref[...])

    @pl.when(j == nj - 1)
    def _():
        y_ref[...] = _rms(x_ref[...] + y_ref[...], gf_ref[...])


def _peer_dense(x, g_ffn, g_final, e1, e2, r2, nrow, u16, v16, *, tm, eb):
    t, dm = x.shape
    nk = PEER_N_KEYS
    ne = u16.shape[0]
    nj = ne // (eb * nk)
    return pl.pallas_call(
        functools.partial(_peer_dense_kernel, tm=tm, eb=eb, nj=nj),
        grid=(t // tm, nj),
        in_specs=[
            pl.BlockSpec((tm, dm), lambda i, j: (i, 0)),
            pl.BlockSpec((1, dm), lambda i, j: (0, 0)),
            pl.BlockSpec((1, dm), lambda i, j: (0, 0)),
            pl.BlockSpec((PEER_HEADS, eb, tm), lambda i, j: (0, j, i)),
            pl.BlockSpec((PEER_HEADS, eb, tm), lambda i, j: (0, j, i)),
            pl.BlockSpec((PEER_HEADS, nk, tm), lambda i, j: (0, 0, i)),
            pl.BlockSpec((PEER_HEADS, nk, tm), lambda i, j: (0, 0, i)),
            pl.BlockSpec((eb * nk, dm), lambda i, j: (j, 0)),
            pl.BlockSpec((eb * nk, dm), lambda i, j: (j, 0)),
        ],
        out_specs=pl.BlockSpec((tm, dm), lambda i, j: (i, 0)),
        out_shape=jax.ShapeDtypeStruct((t, dm), F32),
        scratch_shapes=[
            pltpu.VMEM((tm, dm), BF16),
            pltpu.VMEM((eb * nk, tm), F32),
            pltpu.VMEM((tm, eb * nk), BF16),
        ],
        compiler_params=_cparams("parallel", "arbitrary"),
        name="peer_dense",
    )(x, g_ffn.reshape(1, dm), g_final.reshape(1, dm), e1, nrow, e2, r2, u16, v16)


def _arrange_kernel(w_ref, o_ref, *, pieces, used):
    for src, size, dst in pieces:
        o_ref[dst:dst + size, :] = w_ref[src:src + size, :].astype(BF16)
    o_ref[used:, :] = jnp.zeros((o_ref.shape[0] - used, o_ref.shape[1]), BF16)


def _arrange_w_in(w_in):
    d, n_in = w_in.shape
    w_t = jnp.transpose(w_in)
    sizes = (CONV_DIM, CONV_DIM, 3 * DN_DIM, DN_DIM, DN_HEADS, DN_HEADS, XA_DIM, 2048, 2048, 2048)
    offs = [0]
    for s in sizes:
        offs.append(offs[-1] + s)
    pieces = ((offs[0], offs[2] - offs[0], COL_GLU), (offs[2], offs[3] - offs[2], COL_QKV),
              (offs[3], offs[4] - offs[3], COL_Z), (offs[4], offs[6] - offs[4], COL_AB),
              (offs[6], offs[7] - offs[6], COL_XQ), (offs[7], offs[10] - offs[7], COL_GATES))
    return pl.pallas_call(
        functools.partial(_arrange_kernel, pieces=pieces, used=COL_AB + 2 * DN_HEADS),
        grid=(d // LANES,),
        in_specs=[pl.BlockSpec((n_in, LANES), lambda i: (0, i))],
        out_specs=pl.BlockSpec((PROJ_COLS, LANES), lambda i: (0, i)),
        out_shape=jax.ShapeDtypeStruct((PROJ_COLS, d), BF16),
        compiler_params=_cparams("parallel"),
        name="arrange_w_in",
    )(w_t)


def _layer(x, mem_k, mem_v, conv_buf, qkv_buf, delta_state, w, *, chunk, tiles, g_final):
    bsz, seq, dm = x.shape
    t = bsz * seq
    assert seq % chunk == 0
    tiles = {name: min(size, t) for name, size in tiles.items()}
    x2d = x.reshape(t, dm)
    proj = _norm_matmul(x2d, w["norm_mix"], w["w_in"], tm=tiles["tm_proj"], tn=PROJ_TN, name="in_proj",
                        w_is_transposed=True)
    c, new_conv = _conv_branch(proj, conv_buf, w["conv_dw"], w["conv_dw_b"], w["conv_ln_g"], w["conv_ln_b"],
                               bsz=bsz, seq=seq, tt=tiles["tt_conv"], bb=min(tiles["seqs_per_step"], bsz))
    qkvc, g, beta, new_qkv = _dn_prep(proj, qkv_buf, w["dn_conv_w"], w["dn_a_log"], w["dn_dt_bias"],
                                      bsz=bsz, seq=seq, tt=tiles["tt_dn"], bb=min(tiles["seqs_per_step"], bsz))
    o, new_state = _dn_chunk(qkvc, proj, g, beta, delta_state, w["dn_norm_g"], bsz=bsz, seq=seq, c=chunk,
                             bb=2 if seq == chunk and bsz % 2 == 0 else 1)
    if mem_k.ndim == 4:
        a = _cached_attention(proj, mem_k, mem_v, bsz=bsz, seq=seq, bb=2 if bsz % 2 == 0 else 1)
    else:
        a = _mem_attention(proj, mem_k, mem_v, bsz=bsz, seq=seq, tl=tiles["tl_attn"])
    merged = _merge(c, o, a, proj, w["w_conv_out"], w["w_dn_out"], w["w_xa_out"], tm=tiles["tm_merge"], tn=1024)
    x2 = _out_proj(merged, w["w_out"], x2d, tm=tiles["tm_merge"], tn=1024)
    q = _norm_matmul(x2, w["norm_ffn"], w["w_peer_q"], tm=tiles["tm_merge"], tn=1024, name="peer_q")
    e1, e2, r2, nrow = _peer_route(q, w["peer_keys_1"], w["peer_keys_2"], tm=LANES)
    y = _peer_dense(x2, w["norm_ffn"], g_final, e1, e2, r2, nrow, w["peer_u"], w["peer_v"],
                    tm=tiles["tm_peer"], eb=8)
    return y.reshape(bsz, seq, dm), new_conv, new_qkv, new_state


PROMPT_TILES = dict(tm_proj=1024, tt_conv=256, tt_dn=128, seqs_per_step=1, tl_attn=256, tm_merge=512, tm_peer=512)
SAMPLE_TILES = dict(tm_proj=1024, tt_conv=8, tt_dn=8, seqs_per_step=8, tl_attn=8, tm_merge=512, tm_peer=512)


def kernel(x_prompt, x_sample, mem_prompt, cache_mem_k, cache_mem_v, state_conv, state_qkv_conv, state_delta, norm_mix, norm_mem, w_in, conv_dw, conv_dw_b, conv_ln_g, conv_ln_b, w_conv_out, dn_conv_w, dn_a_log, dn_dt_bias, dn_norm_g, w_dn_out, w_mem_kv, w_xa_out, w_out, norm_ffn, w_peer_q, peer_keys_1, peer_keys_2, peer_u, peer_v, norm_final):
    depth = w_in.shape[0]
    assert depth == 1, "the final norm is fused into the last layer's PEER kernel; one layer is supported"
    bp, sp, dm = x_prompt.shape
    bs, ss, _ = x_sample.shape
    n_mem = mem_prompt.shape[1]
    def layer0(a):
        return a.reshape(a.shape[1:])

    w = dict(
        norm_mix=layer0(norm_mix), w_in=_arrange_w_in(layer0(w_in)), conv_dw=layer0(conv_dw),
        conv_dw_b=layer0(conv_dw_b), conv_ln_g=layer0(conv_ln_g), conv_ln_b=layer0(conv_ln_b),
        w_conv_out=layer0(w_conv_out).astype(BF16), dn_conv_w=layer0(dn_conv_w), dn_a_log=layer0(dn_a_log),
        dn_dt_bias=layer0(dn_dt_bias), dn_norm_g=layer0(dn_norm_g), w_dn_out=layer0(w_dn_out).astype(BF16),
        w_xa_out=layer0(w_xa_out).astype(BF16), w_out=layer0(w_out).astype(BF16), norm_ffn=layer0(norm_ffn),
        w_peer_q=layer0(w_peer_q).astype(BF16), peer_keys_1=layer0(peer_keys_1), peer_keys_2=layer0(peer_keys_2),
        peer_u=layer0(peer_u).astype(BF16), peer_v=layer0(peer_v).astype(BF16),
    )
    mem2d = mem_prompt.reshape(bp * n_mem, dm)
    wkv = layer0(w_mem_kv).astype(BF16)
    mk = _norm_matmul(mem2d, layer0(norm_mem), wkv[:, :XA_DIM], tm=bp * n_mem, tn=512, name="mem_k")
    mv = _norm_matmul(mem2d, layer0(norm_mem), wkv[:, XA_DIM:], tm=bp * n_mem, tn=512, name="mem_v")
    mk = mk.reshape(bp, n_mem, XA_DIM)
    mv = mv.reshape(bp, n_mem, XA_DIM)
    yp, conv_p, qkv_p, delta_p = _layer(
        x_prompt, mk, mv,
        jnp.zeros((bp, CONV_K - 1, CONV_DIM), F32),
        jnp.zeros((bp, DN_CONV_K - 1, 3 * DN_DIM), F32),
        jnp.zeros((bp, DN_HEADS, DN_HEAD_DIM, DN_HEAD_DIM), F32),
        w, chunk=min(DN_CHUNK, sp), tiles=PROMPT_TILES, g_final=norm_final)
    ys, conv_s, qkv_s, delta_s = _layer(
        x_sample, layer0(cache_mem_k), layer0(cache_mem_v),
        layer0(state_conv), layer0(state_qkv_conv), layer0(state_delta),
        w, chunk=min(DN_CHUNK, ss), tiles=SAMPLE_TILES, g_final=norm_final)
    kv_shape = (1, bp, n_mem, XA_HEADS, XA_HEAD_DIM)
    return (yp, ys, mk.reshape(kv_shape), mv.reshape(kv_shape), conv_p[None], qkv_p[None], delta_p[None],
            conv_s[None], qkv_s[None], delta_s[None])
```

```python
import functools

import jax
import jax.numpy as jnp
from jax import lax
from jax.experimental import pallas as pl
from jax.experimental.pallas import tpu as pltpu

F32 = jnp.float32
BF16 = jnp.bfloat16

NORM_EPS = 1e-6
LANES = 128
SUBLANES = 8
VMEM_LIMIT_BYTES = 56 * 1024 * 1024

CONV_DIM = 1024
CONV_K = 31
DN_HEADS = 16
DN_HEAD_DIM = 128
DN_DIM = DN_HEADS * DN_HEAD_DIM
DN_CONV_K = 4
DN_CHUNK = 64
XA_HEADS = 4
XA_HEAD_DIM = 256
XA_DIM = XA_HEADS * XA_HEAD_DIM
PEER_HEADS = 8
PEER_N_KEYS = 128
PEER_DK = 256
PEER_TOPK = 16

COL_QKV = 0
COL_GLU = 3 * DN_DIM
COL_Z = COL_GLU + 2 * CONV_DIM
COL_GATES = COL_Z + DN_DIM
COL_XQ = COL_GATES + 3 * 2048
COL_AB = COL_XQ + XA_DIM
PROJ_TN = 1280
PROJ_COLS = 14 * PROJ_TN


def _cparams(*dims):
    return pltpu.CompilerParams(dimension_semantics=dims, vmem_limit_bytes=VMEM_LIMIT_BYTES)


def _dot(a, b, precision=None):
    return jnp.dot(a, b, preferred_element_type=F32, precision=precision)


def _dot_nt(a, b):
    return lax.dot_general(a, b, (((1,), (1,)), ((), ())), preferred_element_type=F32)


def _dot_tn(a, b):
    return lax.dot_general(a, b, (((0,), (0,)), ((), ())), preferred_element_type=F32)


def _branch_dtype(rows):
    return BF16 if rows % (2 * SUBLANES) == 0 else F32


def _rms(x, g):
    return x * lax.rsqrt(jnp.mean(x * x, axis=-1, keepdims=True) + NORM_EPS) * g


def _silu(x):
    return x * jax.nn.sigmoid(x)


def _norm_matmul_kernel(x_ref, g_ref, w_ref, o_ref, h_ref, *, w_is_transposed):
    @pl.when(pl.program_id(1) == 0)
    def _():
        h_ref[...] = _rms(x_ref[...], g_ref[...]).astype(BF16)

    o_ref[...] = _dot_nt(h_ref[...], w_ref[...]) if w_is_transposed else _dot(h_ref[...], w_ref[...])


def _norm_matmul(x, g, w, *, tm, tn, name, w_is_transposed=False):
    t, d = x.shape
    n = w.shape[0] if w_is_transposed else w.shape[1]
    w_spec = pl.BlockSpec((tn, d), lambda i, j: (j, 0)) if w_is_transposed else pl.BlockSpec((d, tn), lambda i, j: (0, j))
    return pl.pallas_call(
        functools.partial(_norm_matmul_kernel, w_is_transposed=w_is_transposed),
        grid=(t // tm, n // tn),
        in_specs=[
            pl.BlockSpec((tm, d), lambda i, j: (i, 0)),
            pl.BlockSpec((1, d), lambda i, j: (0, 0)),
            w_spec,
        ],
        out_specs=pl.BlockSpec((tm, tn), lambda i, j: (i, j)),
        out_shape=jax.ShapeDtypeStruct((t, n), F32),
        scratch_shapes=[pltpu.VMEM((tm, d), BF16)],
        compiler_params=_cparams("parallel", "arbitrary"),
        name=name,
    )(x, g.reshape(1, d), w)


def _conv_kernel(a_ref, g_ref, buf_ref, w_ref, b_ref, lng_ref, lnb_ref, c_ref, nbuf_ref, xp_ref, sh_ref, y_ref,
                 *, tt, nt, bb):
    i = pl.program_id(1)
    head = 32
    off = head - (CONV_K - 1)

    for bi in range(bb):
        rows_b = slice(bi * tt, (bi + 1) * tt)

        def load_history(bi=bi):
            xp_ref[0:off, :] = jnp.zeros((off, CONV_DIM), F32)
            xp_ref[off:head, :] = buf_ref[bi]

        if nt == 1:
            load_history()
        else:
            pl.when(i == 0)(load_history)

        xp_ref[head:head + tt, :] = a_ref[rows_b, :] * jax.nn.sigmoid(g_ref[rows_b, :])

        span = tt + head - SUBLANES
        for s in range(1, SUBLANES):
            sh_ref[s - 1] = xp_ref[pl.ds(s, span), :]

        rt = min(32, tt)
        cw = 512
        for r in range(tt // rt):
            for cb in range(CONV_DIM // cw):
                cs = slice(cb * cw, (cb + 1) * cw)
                acc = jnp.zeros((rt, cw), F32)
                for j in range(CONV_K):
                    s = (off + j) % SUBLANES
                    rows = pl.ds(r * rt + off + j - s, rt)
                    tap = xp_ref[rows, cs] if s == 0 else sh_ref[s - 1, rows, cs]
                    acc = acc + w_ref[j:j + 1, cs] * tap
                y_ref[r * rt:(r + 1) * rt, cs] = acc + b_ref[:, cs]

        y = y_ref[...]
        mu = jnp.mean(y, axis=-1, keepdims=True)
        yc = y - mu
        var = jnp.mean(yc * yc, axis=-1, keepdims=True)
        c_ref[rows_b, :] = _silu(yc * lax.rsqrt(var + NORM_EPS) * lng_ref[...] + lnb_ref[...]).astype(c_ref.dtype)

        tail = xp_ref[tt:tt + head, :]
        if nt == 1:
            nbuf_ref[bi] = tail[off:, :]
        else:
            def store_history(tail=tail, bi=bi):
                nbuf_ref[bi] = tail[off:, :]

            pl.when(i == nt - 1)(store_history)
            xp_ref[0:head, :] = tail


def _conv_branch(proj, buf, w, b, lng, lnb, *, bsz, seq, tt, bb):
    nt = seq // tt
    assert bb == 1 or nt == 1
    ca = COL_GLU // CONV_DIM
    row = lambda b_, i: b_ * nt + i
    vec = lambda v: v.reshape(1, CONV_DIM)
    return pl.pallas_call(
        functools.partial(_conv_kernel, tt=tt, nt=nt, bb=bb),
        grid=(bsz // bb, nt),
        in_specs=[
            pl.BlockSpec((bb * tt, CONV_DIM), lambda b_, i: (row(b_, i), ca)),
            pl.BlockSpec((bb * tt, CONV_DIM), lambda b_, i: (row(b_, i), ca + 1)),
            pl.BlockSpec((bb, CONV_K - 1, CONV_DIM), lambda b_, i: (b_, 0, 0)),
            pl.BlockSpec((CONV_K, CONV_DIM), lambda b_, i: (0, 0)),
            pl.BlockSpec((1, CONV_DIM), lambda b_, i: (0, 0)),
            pl.BlockSpec((1, CONV_DIM), lambda b_, i: (0, 0)),
            pl.BlockSpec((1, CONV_DIM), lambda b_, i: (0, 0)),
        ],
        out_specs=[
            pl.BlockSpec((bb * tt, CONV_DIM), lambda b_, i: (row(b_, i), 0)),
            pl.BlockSpec((bb, CONV_K - 1, CONV_DIM), lambda b_, i: (b_, 0, 0)),
        ],
        out_shape=[
            jax.ShapeDtypeStruct((bsz * seq, CONV_DIM), _branch_dtype(tt)),
            jax.ShapeDtypeStruct((bsz, CONV_K - 1, CONV_DIM), F32),
        ],
        scratch_shapes=[
            pltpu.VMEM((tt + 32, CONV_DIM), F32),
            pltpu.VMEM((SUBLANES - 1, tt + 32 - SUBLANES, CONV_DIM), F32),
            pltpu.VMEM((tt, CONV_DIM), F32),
        ],
        compiler_params=_cparams("parallel", "arbitrary"),
        name="conv_branch",
    )(proj, proj, buf, w, vec(b), vec(lng), vec(lnb))


def _dn_prep_kernel(x_ref, ab_ref, buf_ref, w_ref, alog_ref, dtb_ref, o_ref, g_ref, beta_ref, nbuf_ref, xp_ref,
                    *, tt, nt, bb):
    i = pl.program_id(1)
    head = SUBLANES
    off = head - (DN_CONV_K - 1)

    for bi in range(bb):
        rows_b = slice(bi * tt, (bi + 1) * tt)

        def load_history(bi=bi):
            xp_ref[0:off, :] = jnp.zeros((off, 3 * DN_DIM), F32)
            xp_ref[off:head, :] = buf_ref[bi]

        if nt == 1:
            load_history()
        else:
            pl.when(i == 0)(load_history)

        xp_ref[head:head + tt, :] = x_ref[rows_b, :]

        for blk in range(3 * DN_HEADS):
            cs = slice(blk * DN_HEAD_DIM, (blk + 1) * DN_HEAD_DIM)
            acc = jnp.zeros((tt, DN_HEAD_DIM), F32)
            for j in range(DN_CONV_K):
                acc = acc + w_ref[j:j + 1, cs] * xp_ref[pl.ds(off + j, tt), cs]
            y = _silu(acc)
            if blk < 2 * DN_HEADS:
                y = y * lax.rsqrt(jnp.sum(y * y, axis=-1, keepdims=True) + NORM_EPS)
                if blk < DN_HEADS:
                    y = y * (DN_HEAD_DIM ** -0.5)
            o_ref[rows_b, cs] = y

        tail = xp_ref[tt:tt + head, :]
        if nt == 1:
            nbuf_ref[bi] = tail[off:, :]
        else:
            def store_history(tail=tail, bi=bi):
                nbuf_ref[bi] = tail[off:, :]

            pl.when(i == nt - 1)(store_history)
            xp_ref[0:head, :] = tail

    ab = ab_ref[...]
    a_raw = ab[:, 0:DN_HEADS]
    b_raw = ab[:, DN_HEADS:2 * DN_HEADS]
    sp_in = a_raw + dtb_ref[...]
    softplus = jnp.maximum(sp_in, 0.0) + jnp.log1p(jnp.exp(-jnp.abs(sp_in)))
    g_ref[...] = -jnp.exp(alog_ref[...]) * softplus
    beta_ref[...] = jax.nn.sigmoid(b_raw)


def _dn_prep(proj, buf, w, a_log, dt_bias, *, bsz, seq, tt, bb):
    nt = seq // tt
    assert bb == 1 or nt == 1
    dq = 3 * DN_DIM
    row = lambda b_, i: b_ * nt + i
    return pl.pallas_call(
        functools.partial(_dn_prep_kernel, tt=tt, nt=nt, bb=bb),
        grid=(bsz // bb, nt),
        in_specs=[
            pl.BlockSpec((bb * tt, dq), lambda b_, i: (row(b_, i), COL_QKV // dq)),
            pl.BlockSpec((bb * tt, LANES), lambda b_, i: (row(b_, i), COL_AB // LANES)),
            pl.BlockSpec((bb, DN_CONV_K - 1, dq), lambda b_, i: (b_, 0, 0)),
            pl.BlockSpec((DN_CONV_K, dq), lambda b_, i: (0, 0)),
            pl.BlockSpec((1, DN_HEADS), lambda b_, i: (0, 0)),
            pl.BlockSpec((1, DN_HEADS), lambda b_, i: (0, 0)),
        ],
        out_specs=[
            pl.BlockSpec((bb * tt, dq), lambda b_, i: (row(b_, i), 0)),
            pl.BlockSpec((bb * tt, DN_HEADS), lambda b_, i: (row(b_, i), 0)),
            pl.BlockSpec((bb * tt, DN_HEADS), lambda b_, i: (row(b_, i), 0)),
            pl.BlockSpec((bb, DN_CONV_K - 1, dq), lambda b_, i: (b_, 0, 0)),
        ],
        out_shape=[
            jax.ShapeDtypeStruct((bsz * seq, dq), F32),
            jax.ShapeDtypeStruct((bsz * seq, DN_HEADS), F32),
            jax.ShapeDtypeStruct((bsz * seq, DN_HEADS), F32),
            jax.ShapeDtypeStruct((bsz, DN_CONV_K - 1, dq), F32),
        ],
        scratch_shapes=[pltpu.VMEM((tt + SUBLANES, dq), F32)],
        compiler_params=_cparams("parallel", "arbitrary"),
        name="dn_prep",
    )(proj, proj, buf, w, a_log.reshape(1, DN_HEADS), dt_bias.reshape(1, DN_HEADS))


def _run_interleaved(gens):
    while gens:
        alive = []
        for g in gens:
            try:
                next(g)
                alive.append(g)
            except StopIteration:
                pass
        gens = alive


def _unit_lower_inverse(a, masks):
    eye, diag_block, off_blocks = masks
    ad = jnp.where(diag_block, a, 0.0)
    def mm(p, q):
        return _dot(p.astype(BF16), q.astype(BF16))

    a2 = mm(ad, ad)
    yield
    a4 = mm(a2, a2)
    x = mm(eye - ad, eye + a2)
    yield
    x = mm(x, eye + a4)
    yield
    for off_block in off_blocks:
        y = mm(x, jnp.where(off_block, a, 0.0))
        yield
        x = x - mm(y, x)
        yield
    return x


def _inverse_masks(c):
    ri = lax.broadcasted_iota(jnp.int32, (c, c), 0)
    ci = lax.broadcasted_iota(jnp.int32, (c, c), 1)
    eye = (ri == ci).astype(F32)
    diag_block = (ri >> 3) == (ci >> 3)
    off_blocks = []
    shift = 3
    while (1 << shift) < c:
        off_blocks.append(((ri >> (shift + 1)) == (ci >> (shift + 1))) & ((ri >> shift) != (ci >> shift)))
        shift += 1
    return eye, diag_block, off_blocks


def _dn_head(q, k, v, z, gcol, beta, grow, s, ng, tri, inv_masks, c):
    incl, strict, upper = tri
    gc_col = jnp.sum(jnp.where(incl, grow, 0.0), axis=1, keepdims=True)
    gc_row = jnp.sum(jnp.where(upper, gcol, 0.0), axis=0, keepdims=True)
    decay_incl = jnp.exp(jnp.where(incl, gc_col - gc_row, -jnp.inf))
    decay_strict = jnp.where(strict, decay_incl, 0.0)

    kb = k * beta
    kq = _dot_nt(jnp.concatenate([kb, q], axis=0).astype(BF16), k.astype(BF16))
    yield
    t_inv = yield from _unit_lower_inverse(kq[:c] * decay_strict, inv_masks)
    e_gc = jnp.exp(gc_col)
    rhs = jnp.concatenate([v * beta, kb * e_gc], axis=1)
    sol = _dot(t_inv.astype(BF16), rhs.astype(BF16))
    yield
    value = sol[:, :DN_HEAD_DIM]
    k_cum = sol[:, DN_HEAD_DIM:]
    qk = kq[c:] * decay_incl
    gc_last = gc_col[c - 1:c, :]
    k_dec = k * jnp.exp(gc_last - gc_col)

    ks = _dot(jnp.concatenate([k_cum, q * e_gc], axis=0).astype(BF16), s.astype(BF16))
    yield
    u16 = (value - ks[:c]).astype(BF16)
    o = ks[c:] + _dot(qk.astype(BF16), u16)
    s_new = s * jnp.exp(gc_last) + _dot_tn(k_dec.astype(BF16), u16)
    return _rms(o, ng) * _silu(z), s_new


def _dn_chunk_kernel(q_ref, k_ref, v_ref, z_ref, g_ref, beta_ref, grow_ref, s0_ref, ng_ref, o_ref, s_ref, *, c, bb):
    @pl.when(pl.program_id(1) == 0)
    def _():
        s_ref[...] = s0_ref[...]

    ri = lax.broadcasted_iota(jnp.int32, (c, c), 0)
    ci = lax.broadcasted_iota(jnp.int32, (c, c), 1)
    tri = (ri >= ci, ri > ci, ri <= ci)
    inv_masks = _inverse_masks(c)
    ng = ng_ref[...]
    d = DN_HEAD_DIM

    def head(bi, h):
        cs = slice(h * d, (h + 1) * d)
        rs = slice(bi * c, (bi + 1) * c)
        o, s_new = yield from _dn_head(q_ref[rs, cs], k_ref[rs, cs], v_ref[rs, cs], z_ref[rs, cs], g_ref[rs, h:h + 1],
                                       beta_ref[rs, h:h + 1], grow_ref[bi, 0, h:h + 1, :], s_ref[bi, h], ng, tri,
                                       inv_masks, c)
        o_ref[rs, cs] = o.astype(o_ref.dtype)
        s_ref[bi, h] = s_new

    _run_interleaved([head(bi, h) for bi in range(bb) for h in range(DN_HEADS)])


def _dn_chunk(qkvc, proj, g, beta, s0, norm_g, *, bsz, seq, c, bb):
    n = seq // c
    assert bb == 1 or n == 1
    h_ = DN_HEADS
    d = DN_HEAD_DIM
    grow = jnp.transpose(g.reshape(bsz, n, c, h_), (0, 1, 3, 2))
    row = lambda b_, n_: (b_ * n + n_)
    return pl.pallas_call(
        functools.partial(_dn_chunk_kernel, c=c, bb=bb),
        grid=(bsz // bb, n),
        in_specs=[
            pl.BlockSpec((bb * c, DN_DIM), lambda b_, n_: (row(b_, n_), 0)),
            pl.BlockSpec((bb * c, DN_DIM), lambda b_, n_: (row(b_, n_), 1)),
            pl.BlockSpec((bb * c, DN_DIM), lambda b_, n_: (row(b_, n_), 2)),
            pl.BlockSpec((bb * c, DN_DIM), lambda b_, n_: (row(b_, n_), COL_Z // DN_DIM)),
            pl.BlockSpec((bb * c, h_), lambda b_, n_: (row(b_, n_), 0)),
            pl.BlockSpec((bb * c, h_), lambda b_, n_: (row(b_, n_), 0)),
            pl.BlockSpec((bb, 1, h_, c), lambda b_, n_: (b_, n_, 0, 0)),
            pl.BlockSpec((bb, h_, d, d), lambda b_, n_: (b_, 0, 0, 0)),
            pl.BlockSpec((1, d), lambda b_, n_: (0, 0)),
        ],
        out_specs=[
            pl.BlockSpec((bb * c, DN_DIM), lambda b_, n_: (row(b_, n_), 0)),
            pl.BlockSpec((bb, h_, d, d), lambda b_, n_: (b_, 0, 0, 0)),
        ],
        out_shape=[
            jax.ShapeDtypeStruct((bsz * seq, DN_DIM), _branch_dtype(c)),
            jax.ShapeDtypeStruct((bsz, h_, d, d), F32),
        ],
        compiler_params=_cparams("parallel", "arbitrary"),
        name="dn_chunk",
    )(qkvc, qkvc, qkvc, proj, g, beta, grow, s0, norm_g.reshape(1, d))


def _mem_attn_kernel(q_ref, k_ref, v_ref, o_ref):
    for h in range(XA_HEADS):
        cs = slice(h * XA_HEAD_DIM, (h + 1) * XA_HEAD_DIM)
        s = _dot_nt(q_ref[:, cs].astype(BF16), k_ref[0, :, cs].astype(BF16)) * (XA_HEAD_DIM ** -0.5)
        e = jnp.exp(s - jnp.max(s, axis=-1, keepdims=True))
        p = e / jnp.sum(e, axis=-1, keepdims=True)
        o_ref[:, cs] = _dot(p.astype(BF16), v_ref[0, :, cs].astype(BF16)).astype(o_ref.dtype)


def _mem_attention(proj, mem_k, mem_v, *, bsz, seq, tl):
    nl = seq // tl
    m = mem_k.shape[1]
    return pl.pallas_call(
        _mem_attn_kernel,
        grid=(bsz, nl),
        in_specs=[
            pl.BlockSpec((tl, XA_DIM), lambda b_, i: (b_ * nl + i, COL_XQ // XA_DIM)),
            pl.BlockSpec((1, m, XA_DIM), lambda b_, i: (b_, 0, 0)),
            pl.BlockSpec((1, m, XA_DIM), lambda b_, i: (b_, 0, 0)),
        ],
        out_specs=pl.BlockSpec((tl, XA_DIM), lambda b_, i: (b_ * nl + i, 0)),
        out_shape=jax.ShapeDtypeStruct((bsz * seq, XA_DIM), _branch_dtype(tl)),
        compiler_params=_cparams("parallel", "arbitrary"),
        name="mem_attention",
    )(proj, mem_k, mem_v)


def _cached_attn_kernel(q_ref, k_ref, v_ref, o_ref, *, tl, bb):
    halves = XA_HEAD_DIM // LANES
    group = halves * XA_HEADS
    rows = XA_HEADS * tl
    n = k_ref.shape[1]
    col = lax.broadcasted_iota(jnp.int32, (rows, n), 1)
    row = lax.broadcasted_iota(jnp.int32, (rows, n), 0)
    assert group & (group - 1) == 0 and tl & (tl - 1) == 0
    own = (col & (group - 1)) == (row >> (tl.bit_length() - 1))
    for bi in range(bb):
        tok = slice(bi * tl, (bi + 1) * tl)
        q2 = jnp.concatenate([q_ref[tok, h * XA_HEAD_DIM + c * LANES:h * XA_HEAD_DIM + (c + 1) * LANES]
                              for c in range(halves) for h in range(XA_HEADS)], axis=0)
        s_all = _dot_nt(q2.astype(BF16), k_ref[bi].astype(BF16))
        s = s_all[:rows]
        for c in range(1, halves):
            s = s + pltpu.roll(s_all[c * rows:(c + 1) * rows], n - c * XA_HEADS, axis=1)
        s = jnp.where(own, s * (XA_HEAD_DIM ** -0.5), -jnp.inf)
        e = jnp.exp(s - jnp.max(s, axis=-1, keepdims=True))
        p = (e / jnp.sum(e, axis=-1, keepdims=True)).astype(BF16)
        p_all = jnp.concatenate([p] + [pltpu.roll(p.astype(F32), c * XA_HEADS, axis=1).astype(BF16)
                                       for c in range(1, halves)], axis=0)
        o = _dot(p_all, v_ref[bi].astype(BF16))
        for c in range(halves):
            for h in range(XA_HEADS):
                r0 = (c * XA_HEADS + h) * tl
                o_ref[tok, h * XA_HEAD_DIM + c * LANES:h * XA_HEAD_DIM + (c + 1) * LANES] = o[r0:r0 + tl]


def _cached_attention(proj, cache_k, cache_v, *, bsz, seq, bb):
    m = cache_k.shape[1]
    halves = XA_HEAD_DIM // LANES
    rows = m * halves * XA_HEADS

    def stored_view(a):
        a = a.reshape(bsz, m, XA_HEADS, halves, LANES)
        return jnp.transpose(a, (0, 1, 3, 2, 4)).reshape(bsz, rows, LANES)

    return pl.pallas_call(
        functools.partial(_cached_attn_kernel, tl=seq, bb=bb),
        grid=(bsz // bb,),
        in_specs=[
            pl.BlockSpec((bb * seq, XA_DIM), lambda b_: (b_, COL_XQ // XA_DIM)),
            pl.BlockSpec((bb, rows, LANES), lambda b_: (b_, 0, 0)),
            pl.BlockSpec((bb, rows, LANES), lambda b_: (b_, 0, 0)),
        ],
        out_specs=pl.BlockSpec((bb * seq, XA_DIM), lambda b_: (b_, 0)),
        out_shape=jax.ShapeDtypeStruct((bsz * seq, XA_DIM), F32),
        compiler_params=_cparams("parallel"),
        name="cached_attention",
    )(proj, stored_view(cache_k), stored_view(cache_v))


def _merge_kernel(c_ref, o_ref, a_ref, gc_ref, gd_ref, gx_ref, wc_ref, wd_ref, wx_ref, m_ref):
    yc = _dot(c_ref[...].astype(BF16), wc_ref[...])
    yd = _dot(o_ref[...].astype(BF16), wd_ref[...])
    yx = _dot(a_ref[...].astype(BF16), wx_ref[...])
    m_ref[...] = (jax.nn.sigmoid(gc_ref[...]) * yc + jax.nn.sigmoid(gd_ref[...]) * yd
                  + jax.nn.sigmoid(gx_ref[...]) * yx).astype(BF16)


def _merge(c, o, a, proj, wc, wd, wx, *, tm, tn):
    t = c.shape[0]
    dm = wc.shape[1]
    gcol = COL_GATES // tn
    gstep = dm // tn
    return pl.pallas_call(
        _merge_kernel,
        grid=(t // tm, dm // tn),
        in_specs=[
            pl.BlockSpec((tm, CONV_DIM), lambda i, j: (i, 0)),
            pl.BlockSpec((tm, DN_DIM), lambda i, j: (i, 0)),
            pl.BlockSpec((tm, XA_DIM), lambda i, j: (i, 0)),
            pl.BlockSpec((tm, tn), lambda i, j: (i, gcol + j)),
            pl.BlockSpec((tm, tn), lambda i, j: (i, gcol + gstep + j)),
            pl.BlockSpec((tm, tn), lambda i, j: (i, gcol + 2 * gstep + j)),
            pl.BlockSpec((CONV_DIM, tn), lambda i, j: (0, j)),
            pl.BlockSpec((DN_DIM, tn), lambda i, j: (0, j)),
            pl.BlockSpec((XA_DIM, tn), lambda i, j: (0, j)),
        ],
        out_specs=pl.BlockSpec((tm, tn), lambda i, j: (i, j)),
        out_shape=jax.ShapeDtypeStruct((t, dm), BF16),
        compiler_params=_cparams("parallel", "arbitrary"),
        name="merge",
    )(c, o, a, proj, proj, proj, wc, wd, wx)


def _out_proj_kernel(m_ref, w_ref, x_ref, o_ref):
    o_ref[...] = x_ref[...] + _dot(m_ref[...], w_ref[...])


def _out_proj(merged, w, x, *, tm, tn):
    t, dm = x.shape
    return pl.pallas_call(
        _out_proj_kernel,
        grid=(t // tm, dm // tn),
        in_specs=[
            pl.BlockSpec((tm, dm), lambda i, j: (i, 0)),
            pl.BlockSpec((dm, tn), lambda i, j: (0, j)),
            pl.BlockSpec((tm, tn), lambda i, j: (i, j)),
        ],
        out_specs=pl.BlockSpec((tm, tn), lambda i, j: (i, j)),
        out_shape=jax.ShapeDtypeStruct((t, dm), F32),
        compiler_params=_cparams("parallel", "arbitrary"),
        name="out_proj",
    )(merged, w, x)


def _top16(s, tm):
    key = lax.broadcasted_iota(jnp.int32, s.shape, 0).astype(F32)
    work = s
    rank = jnp.full(s.shape, float(PEER_TOPK), F32)
    vals = []
    for a in range(PEER_TOPK):
        m = jnp.max(work, axis=0, keepdims=True)
        idx = jnp.min(jnp.where(work == m, key, float(PEER_N_KEYS)), axis=0, keepdims=True)
        sel = key == idx
        rank = jnp.where(sel, float(a), rank)
        work = jnp.where(sel, -jnp.inf, work)
        vals.append(m)
    return vals, rank


def _peer_route_kernel(q_ref, k1_ref, k2_ref, e1_ref, e2_ref, r2_ref, nrow_ref, *, tm, heads):
    k1 = k1_ref[...].astype(BF16)
    k2 = k2_ref[...].astype(BF16)
    for hh in range(heads):
        q = q_ref[:, hh * PEER_DK:(hh + 1) * PEER_DK]
        e1_ref[hh], e2_ref[hh], r2_ref[hh], nrow_ref[hh] = _route_head(q, k1, k2, tm)


def _route_head(q, k1, k2, tm):
    half = PEER_DK // 2
    s1 = _dot_nt(k1, q[:, :half].astype(BF16))
    s2 = _dot_nt(k2, q[:, half:].astype(BF16))
    t1, rank1 = _top16(s1, tm)
    t2, rank2 = _top16(s2, tm)

    k = PEER_TOPK
    cands = [(a, b) for a in range(k) for b in range(k) if (a + 1) * (b + 1) <= k]
    no_pos = float(k * k)
    work, pos = [], []
    for g in range(0, len(cands), SUBLANES):
        group = cands[g:g + SUBLANES]
        pad = SUBLANES - len(group)
        work.append(jnp.concatenate([t1[a] + t2[b] for a, b in group]
                                    + [jnp.full((1, tm), -jnp.inf, F32)] * pad, axis=0))
        pos.append(jnp.concatenate([jnp.full((1, tm), float(a * k + b), F32) for a, b in group]
                                   + [jnp.full((1, tm), no_pos, F32)] * pad, axis=0))
    row_id = lax.broadcasted_iota(jnp.int32, (k, tm), 0).astype(F32)
    n_won = jnp.zeros((k, tm), F32)
    tops = []
    for _ in range(k):
        m = functools.reduce(jnp.maximum, work)
        m = jnp.max(m, axis=0, keepdims=True)
        p = functools.reduce(jnp.minimum, [jnp.where(w_ == m, p_, no_pos) for w_, p_ in zip(work, pos)])
        p = jnp.min(p, axis=0, keepdims=True)
        work = [jnp.where(p_ == p, -jnp.inf, w_) for w_, p_ in zip(work, pos)]
        n_won = n_won + (row_id == jnp.floor(p * (1.0 / k))).astype(F32)
        tops.append(m)
    zsum = jnp.zeros((1, tm), F32)
    for m in tops:
        zsum = zsum + jnp.exp(m - tops[0])
    nrow = jnp.zeros(s1.shape, F32)
    for a in range(k):
        nrow = jnp.where(rank1 == float(a), n_won[a:a + 1, :], nrow)

    return jnp.exp(s1 - t1[0]) / zsum, jnp.exp(s2 - t2[0]), rank2, nrow


def _peer_route(q, k1, k2, *, tm):
    t = q.shape[0]
    nk = PEER_N_KEYS
    out = jax.ShapeDtypeStruct((PEER_HEADS, nk, t), F32)
    heads = 8
    spec = pl.BlockSpec((heads, nk, tm), lambda i, h: (h, 0, i))
    return pl.pallas_call(
        functools.partial(_peer_route_kernel, tm=tm, heads=heads),
        grid=(t // tm, PEER_HEADS // heads),
        in_specs=[
            pl.BlockSpec((tm, heads * PEER_DK), lambda i, h: (i, h)),
            pl.BlockSpec((nk, PEER_DK // 2), lambda i, h: (0, 0)),
            pl.BlockSpec((nk, PEER_DK // 2), lambda i, h: (0, 0)),
        ],
        out_specs=[spec, spec, spec, spec],
        out_shape=[out, out, out, out],
        compiler_params=_cparams("parallel", "arbitrary"),
        name="peer_route",
    )(q, k1, k2)


def _gelu_tanh(x):
    c = 0.7978845608028654
    return x * (0.5 + 0.5 * jnp.tanh(x * (c + (c * 0.044715) * (x * x))))


def _peer_dense_kernel(x_ref, gn_ref, gf_ref, e1_ref, nrow_ref, e2_ref, r2_ref, u_ref, v_ref, y_ref,
                       h_ref, s_ref, p_ref, *, tm, eb, nj):
    j = pl.program_id(1)
    nk = PEER_N_KEYS

    @pl.when(j == 0)
    def _():
        h_ref[...] = _rms(x_ref[...], gn_ref[...]).astype(BF16)
        y_ref[...] = jnp.zeros_like(y_ref)

    s_ref[...] = _dot_nt(u_ref[...], h_ref[...])
    for bi in range(eb):
        for th in range(tm // LANES):
            ts = slice(th * LANES, (th + 1) * LANES)
            w = jnp.zeros((nk, LANES), F32)
            for h in range(PEER_HEADS):
                sel = r2_ref[h, :, ts] < nrow_ref[h, bi:bi + 1, ts]
                w = jnp.where(sel, w + e1_ref[h, bi:bi + 1, ts] * e2_ref[h, :, ts], w)
            act = _gelu_tanh(s_ref[bi * nk:(bi + 1) * nk, ts])
            p_ref[ts, bi * nk:(bi + 1) * nk] = (w * act).T.astype(BF16)
    y_ref[...] += _dot(p_ref[...], v_ref[...])

    @pl.when(j == nj - 1)
    def _():
        y_ref[...] = _rms(x_ref[...] + y_ref[...], gf_ref[...])


def _peer_dense(x, g_ffn, g_final, e1, e2, r2, nrow, u16, v16, *, tm, eb):
    t, dm = x.shape
    nk = PEER_N_KEYS
    ne = u16.shape[0]
    nj = ne // (eb * nk)
    return pl.pallas_call(
        functools.partial(_peer_dense_kernel, tm=tm, eb=eb, nj=nj),
        grid=(t // tm, nj),
        in_specs=[
            pl.BlockSpec((tm, dm), lambda i, j: (i, 0)),
            pl.BlockSpec((1, dm), lambda i, j: (0, 0)),
            pl.BlockSpec((1, dm), lambda i, j: (0, 0)),
            pl.BlockSpec((PEER_HEADS, eb, tm), lambda i, j: (0, j, i)),
            pl.BlockSpec((PEER_HEADS, eb, tm), lambda i, j: (0, j, i)),
            pl.BlockSpec((PEER_HEADS, nk, tm), lambda i, j: (0, 0, i)),
            pl.BlockSpec((PEER_HEADS, nk, tm), lambda i, j: (0, 0, i)),
            pl.BlockSpec((eb * nk, dm), lambda i, j: (j, 0)),
            pl.BlockSpec((eb * nk, dm), lambda i, j: (j, 0)),
        ],
        out_specs=pl.BlockSpec((tm, dm), lambda i, j: (i, 0)),
        out_shape=jax.ShapeDtypeStruct((t, dm), F32),
        scratch_shapes=[
            pltpu.VMEM((tm, dm), BF16),
            pltpu.VMEM((eb * nk, tm), F32),
            pltpu.VMEM((tm, eb * nk), BF16),
        ],
        compiler_params=_cparams("parallel", "arbitrary"),
        name="peer_dense",
    )(x, g_ffn.reshape(1, dm), g_final.reshape(1, dm), e1, nrow, e2, r2, u16, v16)


def _arrange_kernel(w_ref, o_ref, *, pieces, used):
    for src, size, dst in pieces:
        o_ref[dst:dst + size, :] = w_ref[src:src + size, :].astype(BF16)
    o_ref[used:, :] = jnp.zeros((o_ref.shape[0] - used, o_ref.shape[1]), BF16)


def _arrange_w_in(w_in):
    d, n_in = w_in.shape
    w_t = jnp.transpose(w_in)
    sizes = (CONV_DIM, CONV_DIM, 3 * DN_DIM, DN_DIM, DN_HEADS, DN_HEADS, XA_DIM, 2048, 2048, 2048)
    offs = [0]
    for s in sizes:
        offs.append(offs[-1] + s)
    pieces = ((offs[0], offs[2] - offs[0], COL_GLU), (offs[2], offs[3] - offs[2], COL_QKV),
              (offs[3], offs[4] - offs[3], COL_Z), (offs[4], offs[6] - offs[4], COL_AB),
              (offs[6], offs[7] - offs[6], COL_XQ), (offs[7], offs[10] - offs[7], COL_GATES))
    return pl.pallas_call(
        functools.partial(_arrange_kernel, pieces=pieces, used=COL_AB + 2 * DN_HEADS),
        grid=(d // LANES,),
        in_specs=[pl.BlockSpec((n_in, LANES), lambda i: (0, i))],
        out_specs=pl.BlockSpec((PROJ_COLS, LANES), lambda i: (0, i)),
        out_shape=jax.ShapeDtypeStruct((PROJ_COLS, d), BF16),
        compiler_params=_cparams("parallel"),
        name="arrange_w_in",
    )(w_t)


def _layer(x, mem_k, mem_v, conv_buf, qkv_buf, delta_state, w, *, chunk, tiles, g_final):
    bsz, seq, dm = x.shape
    t = bsz * seq
    assert seq % chunk == 0
    tiles = {name: min(size, t) for name, size in tiles.items()}
    x2d = x.reshape(t, dm)
    proj = _norm_matmul(x2d, w["norm_mix"], w["w_in"], tm=tiles["tm_proj"], tn=PROJ_TN, name="in_proj",
                        w_is_transposed=True)
    c, new_conv = _conv_branch(proj, conv_buf, w["conv_dw"], w["conv_dw_b"], w["conv_ln_g"], w["conv_ln_b"],
                               bsz=bsz, seq=seq, tt=tiles["tt_conv"], bb=min(tiles["seqs_per_step"], bsz))
    qkvc, g, beta, new_qkv = _dn_prep(proj, qkv_buf, w["dn_conv_w"], w["dn_a_log"], w["dn_dt_bias"],
                                      bsz=bsz, seq=seq, tt=tiles["tt_dn"], bb=min(tiles["seqs_per_step"], bsz))
    o, new_state = _dn_chunk(qkvc, proj, g, beta, delta_state, w["dn_norm_g"], bsz=bsz, seq=seq, c=chunk,
                             bb=2 if seq == chunk and bsz % 2 == 0 else 1)
    if mem_k.ndim == 4:
        a = _cached_attention(proj, mem_k, mem_v, bsz=bsz, seq=seq, bb=2 if bsz % 2 == 0 else 1)
    else:
        a = _mem_attention(proj, mem_k, mem_v, bsz=bsz, seq=seq, tl=tiles["tl_attn"])
    merged = _merge(c, o, a, proj, w["w_conv_out"], w["w_dn_out"], w["w_xa_out"], tm=tiles["tm_mix"], tn=512)
    x2 = _out_proj(merged, w["w_out"], x2d, tm=tiles["tm_merge"], tn=1024)
    q = _norm_matmul(x2, w["norm_ffn"], w["w_peer_q"], tm=tiles["tm_merge"], tn=1024, name="peer_q")
    e1, e2, r2, nrow = _peer_route(q, w["peer_keys_1"], w["peer_keys_2"], tm=LANES)
    y = _peer_dense(x2, w["norm_ffn"], g_final, e1, e2, r2, nrow, w["peer_u"], w["peer_v"],
                    tm=tiles["tm_peer"], eb=8)
    return y.reshape(bsz, seq, dm), new_conv, new_qkv, new_state


PROMPT_TILES = dict(tm_proj=1024, tt_conv=256, tt_dn=128, seqs_per_step=1, tl_attn=256, tm_mix=1024, tm_merge=512,
                    tm_peer=512)
SAMPLE_TILES = dict(tm_proj=1024, tt_conv=8, tt_dn=8, seqs_per_step=8, tl_attn=8, tm_mix=512, tm_merge=512,
                    tm_peer=512)


def kernel(x_prompt, x_sample, mem_prompt, cache_mem_k, cache_mem_v, state_conv, state_qkv_conv, state_delta, norm_mix, norm_mem, w_in, conv_dw, conv_dw_b, conv_ln_g, conv_ln_b, w_conv_out, dn_conv_w, dn_a_log, dn_dt_bias, dn_norm_g, w_dn_out, w_mem_kv, w_xa_out, w_out, norm_ffn, w_peer_q, peer_keys_1, peer_keys_2, peer_u, peer_v, norm_final):
    depth = w_in.shape[0]
    assert depth == 1, "the final norm is fused into the last layer's PEER kernel; one layer is supported"
    bp, sp, dm = x_prompt.shape
    bs, ss, _ = x_sample.shape
    n_mem = mem_prompt.shape[1]
    def layer0(a):
        return a.reshape(a.shape[1:])

    w = dict(
        norm_mix=layer0(norm_mix), w_in=_arrange_w_in(layer0(w_in)), conv_dw=layer0(conv_dw),
        conv_dw_b=layer0(conv_dw_b), conv_ln_g=layer0(conv_ln_g), conv_ln_b=layer0(conv_ln_b),
        w_conv_out=layer0(w_conv_out).astype(BF16), dn_conv_w=layer0(dn_conv_w), dn_a_log=layer0(dn_a_log),
        dn_dt_bias=layer0(dn_dt_bias), dn_norm_g=layer0(dn_norm_g), w_dn_out=layer0(w_dn_out).astype(BF16),
        w_xa_out=layer0(w_xa_out).astype(BF16), w_out=layer0(w_out).astype(BF16), norm_ffn=layer0(norm_ffn),
        w_peer_q=layer0(w_peer_q).astype(BF16), peer_keys_1=layer0(peer_keys_1), peer_keys_2=layer0(peer_keys_2),
        peer_u=layer0(peer_u).astype(BF16), peer_v=layer0(peer_v).astype(BF16),
    )
    mem2d = mem_prompt.reshape(bp * n_mem, dm)
    wkv = layer0(w_mem_kv).astype(BF16)
    mk = _norm_matmul(mem2d, layer0(norm_mem), wkv[:, :XA_DIM], tm=bp * n_mem, tn=512, name="mem_k")
    mv = _norm_matmul(mem2d, layer0(norm_mem), wkv[:, XA_DIM:], tm=bp * n_mem, tn=512, name="mem_v")
    mk = mk.reshape(bp, n_mem, XA_DIM)
    mv = mv.reshape(bp, n_mem, XA_DIM)
    yp, conv_p, qkv_p, delta_p = _layer(
        x_prompt, mk, mv,
        jnp.zeros((bp, CONV_K - 1, CONV_DIM), F32),
        jnp.zeros((bp, DN_CONV_K - 1, 3 * DN_DIM), F32),
        jnp.zeros((bp, DN_HEADS, DN_HEAD_DIM, DN_HEAD_DIM), F32),
        w, chunk=min(DN_CHUNK, sp), tiles=PROMPT_TILES, g_final=norm_final)
    ys, conv_s, qkv_s, delta_s = _layer(
        x_sample, layer0(cache_mem_k), layer0(cache_mem_v),
        layer0(state_conv), layer0(state_qkv_conv), layer0(state_delta),
        w, chunk=min(DN_CHUNK, ss), tiles=SAMPLE_TILES, g_final=norm_final)
    kv_shape = (1, bp, n_mem, XA_HEADS, XA_HEAD_DIM)
    return (yp, ys, mk.reshape(kv_shape), mv.reshape(kv_shape), conv_p[None], qkv_p[None], delta_p[None],
            conv_s[None], qkv_s[None], delta_s[None])
```

```python
import functools

import jax
import jax.numpy as jnp
from jax import lax
from jax.experimental import pallas as pl
from jax.experimental.pallas import tpu as pltpu

F32 = jnp.float32
BF16 = jnp.bfloat16

NORM_EPS = 1e-6
LANES = 128
SUBLANES = 8
VMEM_LIMIT_BYTES = 56 * 1024 * 1024

CONV_DIM = 1024
CONV_K = 31
DN_HEADS = 16
DN_HEAD_DIM = 128
DN_DIM = DN_HEADS * DN_HEAD_DIM
DN_CONV_K = 4
DN_CHUNK = 64
XA_HEADS = 4
XA_HEAD_DIM = 256
XA_DIM = XA_HEADS * XA_HEAD_DIM
PEER_HEADS = 8
PEER_N_KEYS = 128
PEER_DK = 256
PEER_TOPK = 16

COL_QKV = 0
COL_GLU = 3 * DN_DIM
COL_Z = COL_GLU + 2 * CONV_DIM
COL_GATES = COL_Z + DN_DIM
COL_XQ = COL_GATES + 3 * 2048
COL_AB = COL_XQ + XA_DIM
PROJ_TN = 1280
PROJ_COLS = 14 * PROJ_TN


def _cparams(*dims):
    return pltpu.CompilerParams(dimension_semantics=dims, vmem_limit_bytes=VMEM_LIMIT_BYTES)


def _dot(a, b, precision=None):
    return jnp.dot(a, b, preferred_element_type=F32, precision=precision)


def _dot_nt(a, b):
    return lax.dot_general(a, b, (((1,), (1,)), ((), ())), preferred_element_type=F32)


def _dot_tn(a, b):
    return lax.dot_general(a, b, (((0,), (0,)), ((), ())), preferred_element_type=F32)


def _branch_dtype(rows):
    return BF16 if rows % (2 * SUBLANES) == 0 else F32


def _rms(x, g):
    return x * lax.rsqrt(jnp.mean(x * x, axis=-1, keepdims=True) + NORM_EPS) * g


def _silu(x):
    return x * jax.nn.sigmoid(x)


def _norm_matmul_kernel(x_ref, g_ref, w_ref, o_ref, h_ref, *, w_is_transposed):
    @pl.when(pl.program_id(1) == 0)
    def _():
        h_ref[...] = _rms(x_ref[...], g_ref[...]).astype(BF16)

    o_ref[...] = _dot_nt(h_ref[...], w_ref[...]) if w_is_transposed else _dot(h_ref[...], w_ref[...])


def _norm_matmul(x, g, w, *, tm, tn, name, w_is_transposed=False):
    t, d = x.shape
    n = w.shape[0] if w_is_transposed else w.shape[1]
    w_spec = pl.BlockSpec((tn, d), lambda i, j: (j, 0)) if w_is_transposed else pl.BlockSpec((d, tn), lambda i, j: (0, j))
    return pl.pallas_call(
        functools.partial(_norm_matmul_kernel, w_is_transposed=w_is_transposed),
        grid=(t // tm, n // tn),
        in_specs=[
            pl.BlockSpec((tm, d), lambda i, j: (i, 0)),
            pl.BlockSpec((1, d), lambda i, j: (0, 0)),
            w_spec,
        ],
        out_specs=pl.BlockSpec((tm, tn), lambda i, j: (i, j)),
        out_shape=jax.ShapeDtypeStruct((t, n), F32),
        scratch_shapes=[pltpu.VMEM((tm, d), BF16)],
        compiler_params=_cparams("parallel", "arbitrary"),
        name=name,
    )(x, g.reshape(1, d), w)


def _conv_kernel(a_ref, g_ref, buf_ref, w_ref, b_ref, lng_ref, lnb_ref, c_ref, nbuf_ref, xp_ref, sh_ref, y_ref,
                 *, tt, nt, bb):
    i = pl.program_id(1)
    head = 32
    off = head - (CONV_K - 1)

    for bi in range(bb):
        rows_b = slice(bi * tt, (bi + 1) * tt)

        def load_history(bi=bi):
            xp_ref[0:off, :] = jnp.zeros((off, CONV_DIM), F32)
            xp_ref[off:head, :] = buf_ref[bi]

        if nt == 1:
            load_history()
        else:
            pl.when(i == 0)(load_history)

        xp_ref[head:head + tt, :] = a_ref[rows_b, :] * jax.nn.sigmoid(g_ref[rows_b, :])

        span = tt + head - SUBLANES
        for s in range(1, SUBLANES):
            sh_ref[s - 1] = xp_ref[pl.ds(s, span), :]

        rt = min(32, tt)
        cw = 512
        for r in range(tt // rt):
            for cb in range(CONV_DIM // cw):
                cs = slice(cb * cw, (cb + 1) * cw)
                acc = jnp.zeros((rt, cw), F32)
                for j in range(CONV_K):
                    s = (off + j) % SUBLANES
                    rows = pl.ds(r * rt + off + j - s, rt)
                    tap = xp_ref[rows, cs] if s == 0 else sh_ref[s - 1, rows, cs]
                    acc = acc + w_ref[j:j + 1, cs] * tap
                y_ref[r * rt:(r + 1) * rt, cs] = acc + b_ref[:, cs]

        y = y_ref[...]
        mu = jnp.mean(y, axis=-1, keepdims=True)
        yc = y - mu
        var = jnp.mean(yc * yc, axis=-1, keepdims=True)
        c_ref[rows_b, :] = _silu(yc * lax.rsqrt(var + NORM_EPS) * lng_ref[...] + lnb_ref[...]).astype(c_ref.dtype)

        tail = xp_ref[tt:tt + head, :]
        if nt == 1:
            nbuf_ref[bi] = tail[off:, :]
        else:
            def store_history(tail=tail, bi=bi):
                nbuf_ref[bi] = tail[off:, :]

            pl.when(i == nt - 1)(store_history)
            xp_ref[0:head, :] = tail


def _conv_branch(proj, buf, w, b, lng, lnb, *, bsz, seq, tt, bb):
    nt = seq // tt
    assert bb == 1 or nt == 1
    ca = COL_GLU // CONV_DIM
    row = lambda b_, i: b_ * nt + i
    vec = lambda v: v.reshape(1, CONV_DIM)
    return pl.pallas_call(
        functools.partial(_conv_kernel, tt=tt, nt=nt, bb=bb),
        grid=(bsz // bb, nt),
        in_specs=[
            pl.BlockSpec((bb * tt, CONV_DIM), lambda b_, i: (row(b_, i), ca)),
            pl.BlockSpec((bb * tt, CONV_DIM), lambda b_, i: (row(b_, i), ca + 1)),
            pl.BlockSpec((bb, CONV_K - 1, CONV_DIM), lambda b_, i: (b_, 0, 0)),
            pl.BlockSpec((CONV_K, CONV_DIM), lambda b_, i: (0, 0)),
            pl.BlockSpec((1, CONV_DIM), lambda b_, i: (0, 0)),
            pl.BlockSpec((1, CONV_DIM), lambda b_, i: (0, 0)),
            pl.BlockSpec((1, CONV_DIM), lambda b_, i: (0, 0)),
        ],
        out_specs=[
            pl.BlockSpec((bb * tt, CONV_DIM), lambda b_, i: (row(b_, i), 0)),
            pl.BlockSpec((bb, CONV_K - 1, CONV_DIM), lambda b_, i: (b_, 0, 0)),
        ],
        out_shape=[
            jax.ShapeDtypeStruct((bsz * seq, CONV_DIM), _branch_dtype(tt)),
            jax.ShapeDtypeStruct((bsz, CONV_K - 1, CONV_DIM), F32),
        ],
        scratch_shapes=[
            pltpu.VMEM((tt + 32, CONV_DIM), F32),
            pltpu.VMEM((SUBLANES - 1, tt + 32 - SUBLANES, CONV_DIM), F32),
            pltpu.VMEM((tt, CONV_DIM), F32),
        ],
        compiler_params=_cparams("parallel", "arbitrary"),
        name="conv_branch",
    )(proj, proj, buf, w, vec(b), vec(lng), vec(lnb))


def _dn_prep_kernel(x_ref, ab_ref, buf_ref, w_ref, alog_ref, dtb_ref, o_ref, g_ref, beta_ref, nbuf_ref, xp_ref,
                    *, tt, nt, bb):
    i = pl.program_id(1)
    head = SUBLANES
    off = head - (DN_CONV_K - 1)

    for bi in range(bb):
        rows_b = slice(bi * tt, (bi + 1) * tt)

        def load_history(bi=bi):
            xp_ref[0:off, :] = jnp.zeros((off, 3 * DN_DIM), F32)
            xp_ref[off:head, :] = buf_ref[bi]

        if nt == 1:
            load_history()
        else:
            pl.when(i == 0)(load_history)

        xp_ref[head:head + tt, :] = x_ref[rows_b, :]

        for blk in range(3 * DN_HEADS):
            cs = slice(blk * DN_HEAD_DIM, (blk + 1) * DN_HEAD_DIM)
            acc = jnp.zeros((tt, DN_HEAD_DIM), F32)
            for j in range(DN_CONV_K):
                acc = acc + w_ref[j:j + 1, cs] * xp_ref[pl.ds(off + j, tt), cs]
            y = _silu(acc)
            if blk < 2 * DN_HEADS:
                y = y * lax.rsqrt(jnp.sum(y * y, axis=-1, keepdims=True) + NORM_EPS)
                if blk < DN_HEADS:
                    y = y * (DN_HEAD_DIM ** -0.5)
            o_ref[rows_b, cs] = y

        tail = xp_ref[tt:tt + head, :]
        if nt == 1:
            nbuf_ref[bi] = tail[off:, :]
        else:
            def store_history(tail=tail, bi=bi):
                nbuf_ref[bi] = tail[off:, :]

            pl.when(i == nt - 1)(store_history)
            xp_ref[0:head, :] = tail

    ab = ab_ref[...]
    a_raw = ab[:, 0:DN_HEADS]
    b_raw = ab[:, DN_HEADS:2 * DN_HEADS]
    sp_in = a_raw + dtb_ref[...]
    softplus = jnp.maximum(sp_in, 0.0) + jnp.log1p(jnp.exp(-jnp.abs(sp_in)))
    g_ref[...] = -jnp.exp(alog_ref[...]) * softplus
    beta_ref[...] = jax.nn.sigmoid(b_raw)


def _dn_prep(proj, buf, w, a_log, dt_bias, *, bsz, seq, tt, bb):
    nt = seq // tt
    assert bb == 1 or nt == 1
    dq = 3 * DN_DIM
    row = lambda b_, i: b_ * nt + i
    return pl.pallas_call(
        functools.partial(_dn_prep_kernel, tt=tt, nt=nt, bb=bb),
        grid=(bsz // bb, nt),
        in_specs=[
            pl.BlockSpec((bb * tt, dq), lambda b_, i: (row(b_, i), COL_QKV // dq)),
            pl.BlockSpec((bb * tt, LANES), lambda b_, i: (row(b_, i), COL_AB // LANES)),
            pl.BlockSpec((bb, DN_CONV_K - 1, dq), lambda b_, i: (b_, 0, 0)),
            pl.BlockSpec((DN_CONV_K, dq), lambda b_, i: (0, 0)),
            pl.BlockSpec((1, DN_HEADS), lambda b_, i: (0, 0)),
            pl.BlockSpec((1, DN_HEADS), lambda b_, i: (0, 0)),
        ],
        out_specs=[
            pl.BlockSpec((bb * tt, dq), lambda b_, i: (row(b_, i), 0)),
            pl.BlockSpec((bb * tt, DN_HEADS), lambda b_, i: (row(b_, i), 0)),
            pl.BlockSpec((bb * tt, DN_HEADS), lambda b_, i: (row(b_, i), 0)),
            pl.BlockSpec((bb, DN_CONV_K - 1, dq), lambda b_, i: (b_, 0, 0)),
        ],
        out_shape=[
            jax.ShapeDtypeStruct((bsz * seq, dq), F32),
            jax.ShapeDtypeStruct((bsz * seq, DN_HEADS), F32),
            jax.ShapeDtypeStruct((bsz * seq, DN_HEADS), F32),
            jax.ShapeDtypeStruct((bsz, DN_CONV_K - 1, dq), F32),
        ],
        scratch_shapes=[pltpu.VMEM((tt + SUBLANES, dq), F32)],
        compiler_params=_cparams("parallel", "arbitrary"),
        name="dn_prep",
    )(proj, proj, buf, w, a_log.reshape(1, DN_HEADS), dt_bias.reshape(1, DN_HEADS))


def _run_interleaved(gens):
    while gens:
        alive = []
        for g in gens:
            try:
                next(g)
                alive.append(g)
            except StopIteration:
                pass
        gens = alive


def _unit_lower_inverse(a, masks):
    eye, diag_block, off_blocks = masks
    ad = jnp.where(diag_block, a, 0.0)
    def mm(p, q):
        return _dot(p.astype(BF16), q.astype(BF16))

    a2 = mm(ad, ad)
    yield
    a4 = mm(a2, a2)
    x = mm(eye - ad, eye + a2)
    yield
    x = mm(x, eye + a4)
    yield
    for off_block in off_blocks:
        y = mm(x, jnp.where(off_block, a, 0.0))
        yield
        x = x - mm(y, x)
        yield
    return x


def _inverse_masks(c):
    ri = lax.broadcasted_iota(jnp.int32, (c, c), 0)
    ci = lax.broadcasted_iota(jnp.int32, (c, c), 1)
    eye = (ri == ci).astype(F32)
    diag_block = (ri >> 3) == (ci >> 3)
    off_blocks = []
    shift = 3
    while (1 << shift) < c:
        off_blocks.append(((ri >> (shift + 1)) == (ci >> (shift + 1))) & ((ri >> shift) != (ci >> shift)))
        shift += 1
    return eye, diag_block, off_blocks


def _dn_head(q, k, v, z, gcol, beta, grow, s, ng, tri, inv_masks, c):
    incl, strict, upper = tri
    gc_col = jnp.sum(jnp.where(incl, grow, 0.0), axis=1, keepdims=True)
    gc_row = jnp.sum(jnp.where(upper, gcol, 0.0), axis=0, keepdims=True)
    decay_incl = jnp.exp(jnp.where(incl, gc_col - gc_row, -jnp.inf))
    decay_strict = jnp.where(strict, decay_incl, 0.0)

    kb = k * beta
    kq = _dot_nt(jnp.concatenate([kb, q], axis=0).astype(BF16), k.astype(BF16))
    yield
    t_inv = yield from _unit_lower_inverse(kq[:c] * decay_strict, inv_masks)
    e_gc = jnp.exp(gc_col)
    rhs = jnp.concatenate([v * beta, kb * e_gc], axis=1)
    sol = _dot(t_inv.astype(BF16), rhs.astype(BF16))
    yield
    value = sol[:, :DN_HEAD_DIM]
    k_cum = sol[:, DN_HEAD_DIM:]
    qk = kq[c:] * decay_incl
    gc_last = gc_col[c - 1:c, :]
    k_dec = k * jnp.exp(gc_last - gc_col)

    ks = _dot(jnp.concatenate([k_cum, q * e_gc], axis=0).astype(BF16), s.astype(BF16))
    yield
    u16 = (value - ks[:c]).astype(BF16)
    o = ks[c:] + _dot(qk.astype(BF16), u16)
    s_new = s * jnp.exp(gc_last) + _dot_tn(k_dec.astype(BF16), u16)
    return _rms(o, ng) * _silu(z), s_new


def _dn_chunk_kernel(q_ref, k_ref, v_ref, z_ref, g_ref, beta_ref, grow_ref, s0_ref, ng_ref, o_ref, s_ref, *, c, bb):
    @pl.when(pl.program_id(1) == 0)
    def _():
        s_ref[...] = s0_ref[...]

    ri = lax.broadcasted_iota(jnp.int32, (c, c), 0)
    ci = lax.broadcasted_iota(jnp.int32, (c, c), 1)
    tri = (ri >= ci, ri > ci, ri <= ci)
    inv_masks = _inverse_masks(c)
    ng = ng_ref[...]
    d = DN_HEAD_DIM

    def head(bi, h):
        cs = slice(h * d, (h + 1) * d)
        rs = slice(bi * c, (bi + 1) * c)
        o, s_new = yield from _dn_head(q_ref[rs, cs], k_ref[rs, cs], v_ref[rs, cs], z_ref[rs, cs], g_ref[rs, h:h + 1],
                                       beta_ref[rs, h:h + 1], grow_ref[bi, 0, h:h + 1, :], s_ref[bi, h], ng, tri,
                                       inv_masks, c)
        o_ref[rs, cs] = o.astype(o_ref.dtype)
        s_ref[bi, h] = s_new

    _run_interleaved([head(bi, h) for bi in range(bb) for h in range(DN_HEADS)])


def _dn_chunk(qkvc, proj, g, beta, s0, norm_g, *, bsz, seq, c, bb):
    n = seq // c
    assert bb == 1 or n == 1
    h_ = DN_HEADS
    d = DN_HEAD_DIM
    grow = jnp.transpose(g.reshape(bsz, n, c, h_), (0, 1, 3, 2))
    row = lambda b_, n_: (b_ * n + n_)
    return pl.pallas_call(
        functools.partial(_dn_chunk_kernel, c=c, bb=bb),
        grid=(bsz // bb, n),
        in_specs=[
            pl.BlockSpec((bb * c, DN_DIM), lambda b_, n_: (row(b_, n_), 0)),
            pl.BlockSpec((bb * c, DN_DIM), lambda b_, n_: (row(b_, n_), 1)),
            pl.BlockSpec((bb * c, DN_DIM), lambda b_, n_: (row(b_, n_), 2)),
            pl.BlockSpec((bb * c, DN_DIM), lambda b_, n_: (row(b_, n_), COL_Z // DN_DIM)),
            pl.BlockSpec((bb * c, h_), lambda b_, n_: (row(b_, n_), 0)),
            pl.BlockSpec((bb * c, h_), lambda b_, n_: (row(b_, n_), 0)),
            pl.BlockSpec((bb, 1, h_, c), lambda b_, n_: (b_, n_, 0, 0)),
            pl.BlockSpec((bb, h_, d, d), lambda b_, n_: (b_, 0, 0, 0)),
            pl.BlockSpec((1, d), lambda b_, n_: (0, 0)),
        ],
        out_specs=[
            pl.BlockSpec((bb * c, DN_DIM), lambda b_, n_: (row(b_, n_), 0)),
            pl.BlockSpec((bb, h_, d, d), lambda b_, n_: (b_, 0, 0, 0)),
        ],
        out_shape=[
            jax.ShapeDtypeStruct((bsz * seq, DN_DIM), _branch_dtype(c)),
            jax.ShapeDtypeStruct((bsz, h_, d, d), F32),
        ],
        compiler_params=_cparams("parallel", "arbitrary"),
        name="dn_chunk",
    )(qkvc, qkvc, qkvc, proj, g, beta, grow, s0, norm_g.reshape(1, d))


def _mem_attn_kernel(q_ref, k_ref, v_ref, o_ref):
    for h in range(XA_HEADS):
        cs = slice(h * XA_HEAD_DIM, (h + 1) * XA_HEAD_DIM)
        s = _dot_nt(q_ref[:, cs].astype(BF16), k_ref[0, :, cs].astype(BF16)) * (XA_HEAD_DIM ** -0.5)
        e = jnp.exp(s - jnp.max(s, axis=-1, keepdims=True))
        p = e / jnp.sum(e, axis=-1, keepdims=True)
        o_ref[:, cs] = _dot(p.astype(BF16), v_ref[0, :, cs].astype(BF16)).astype(o_ref.dtype)


def _mem_attention(proj, mem_k, mem_v, *, bsz, seq, tl):
    nl = seq // tl
    m = mem_k.shape[1]
    return pl.pallas_call(
        _mem_attn_kernel,
        grid=(bsz, nl),
        in_specs=[
            pl.BlockSpec((tl, XA_DIM), lambda b_, i: (b_ * nl + i, COL_XQ // XA_DIM)),
            pl.BlockSpec((1, m, XA_DIM), lambda b_, i: (b_, 0, 0)),
            pl.BlockSpec((1, m, XA_DIM), lambda b_, i: (b_, 0, 0)),
        ],
        out_specs=pl.BlockSpec((tl, XA_DIM), lambda b_, i: (b_ * nl + i, 0)),
        out_shape=jax.ShapeDtypeStruct((bsz * seq, XA_DIM), _branch_dtype(tl)),
        compiler_params=_cparams("parallel", "arbitrary"),
        name="mem_attention",
    )(proj, mem_k, mem_v)


def _cached_attn_kernel(q_ref, k_ref, v_ref, o_ref, *, tl, bb):
    halves = XA_HEAD_DIM // LANES
    group = halves * XA_HEADS
    rows = XA_HEADS * tl
    n = k_ref.shape[1]
    col = lax.broadcasted_iota(jnp.int32, (rows, n), 1)
    row = lax.broadcasted_iota(jnp.int32, (rows, n), 0)
    assert group & (group - 1) == 0 and tl & (tl - 1) == 0
    own = (col & (group - 1)) == (row >> (tl.bit_length() - 1))
    for bi in range(bb):
        tok = slice(bi * tl, (bi + 1) * tl)
        q2 = jnp.concatenate([q_ref[tok, h * XA_HEAD_DIM + c * LANES:h * XA_HEAD_DIM + (c + 1) * LANES]
                              for c in range(halves) for h in range(XA_HEADS)], axis=0)
        s_all = _dot_nt(q2.astype(BF16), k_ref[bi].astype(BF16))
        s = s_all[:rows]
        for c in range(1, halves):
            s = s + pltpu.roll(s_all[c * rows:(c + 1) * rows], n - c * XA_HEADS, axis=1)
        s = jnp.where(own, s * (XA_HEAD_DIM ** -0.5), -jnp.inf)
        e = jnp.exp(s - jnp.max(s, axis=-1, keepdims=True))
        p = (e / jnp.sum(e, axis=-1, keepdims=True)).astype(BF16)
        p_all = jnp.concatenate([p] + [pltpu.roll(p.astype(F32), c * XA_HEADS, axis=1).astype(BF16)
                                       for c in range(1, halves)], axis=0)
        o = _dot(p_all, v_ref[bi].astype(BF16))
        for c in range(halves):
            for h in range(XA_HEADS):
                r0 = (c * XA_HEADS + h) * tl
                o_ref[tok, h * XA_HEAD_DIM + c * LANES:h * XA_HEAD_DIM + (c + 1) * LANES] = o[r0:r0 + tl]


def _cached_attention(proj, cache_k, cache_v, *, bsz, seq, bb):
    m = cache_k.shape[1]
    halves = XA_HEAD_DIM // LANES
    rows = m * halves * XA_HEADS

    def stored_view(a):
        a = a.reshape(bsz, m, XA_HEADS, halves, LANES)
        return jnp.transpose(a, (0, 1, 3, 2, 4)).reshape(bsz, rows, LANES)

    return pl.pallas_call(
        functools.partial(_cached_attn_kernel, tl=seq, bb=bb),
        grid=(bsz // bb,),
        in_specs=[
            pl.BlockSpec((bb * seq, XA_DIM), lambda b_: (b_, COL_XQ // XA_DIM)),
            pl.BlockSpec((bb, rows, LANES), lambda b_: (b_, 0, 0)),
            pl.BlockSpec((bb, rows, LANES), lambda b_: (b_, 0, 0)),
        ],
        out_specs=pl.BlockSpec((bb * seq, XA_DIM), lambda b_: (b_, 0)),
        out_shape=jax.ShapeDtypeStruct((bsz * seq, XA_DIM), F32),
        compiler_params=_cparams("parallel"),
        name="cached_attention",
    )(proj, stored_view(cache_k), stored_view(cache_v))


def _merge_kernel(c_ref, o_ref, a_ref, gc_ref, gd_ref, gx_ref, wc_ref, wd_ref, wx_ref, m_ref):
    yc = _dot(c_ref[...].astype(BF16), wc_ref[...])
    yd = _dot(o_ref[...].astype(BF16), wd_ref[...])
    yx = _dot(a_ref[...].astype(BF16), wx_ref[...])
    m_ref[...] = (jax.nn.sigmoid(gc_ref[...]) * yc + jax.nn.sigmoid(gd_ref[...]) * yd
                  + jax.nn.sigmoid(gx_ref[...]) * yx).astype(BF16)


def _merge(c, o, a, proj, wc, wd, wx, *, tm, tn):
    t = c.shape[0]
    dm = wc.shape[1]
    gcol = COL_GATES // tn
    gstep = dm // tn
    return pl.pallas_call(
        _merge_kernel,
        grid=(t // tm, dm // tn),
        in_specs=[
            pl.BlockSpec((tm, CONV_DIM), lambda i, j: (i, 0)),
            pl.BlockSpec((tm, DN_DIM), lambda i, j: (i, 0)),
            pl.BlockSpec((tm, XA_DIM), lambda i, j: (i, 0)),
            pl.BlockSpec((tm, tn), lambda i, j: (i, gcol + j)),
            pl.BlockSpec((tm, tn), lambda i, j: (i, gcol + gstep + j)),
            pl.BlockSpec((tm, tn), lambda i, j: (i, gcol + 2 * gstep + j)),
            pl.BlockSpec((CONV_DIM, tn), lambda i, j: (0, j)),
            pl.BlockSpec((DN_DIM, tn), lambda i, j: (0, j)),
            pl.BlockSpec((XA_DIM, tn), lambda i, j: (0, j)),
        ],
        out_specs=pl.BlockSpec((tm, tn), lambda i, j: (i, j)),
        out_shape=jax.ShapeDtypeStruct((t, dm), BF16),
        compiler_params=_cparams("parallel", "arbitrary"),
        name="merge",
    )(c, o, a, proj, proj, proj, wc, wd, wx)


def _out_proj_kernel(m_ref, w_ref, x_ref, o_ref):
    o_ref[...] = x_ref[...] + _dot(m_ref[...], w_ref[...])


def _out_proj(merged, w, x, *, tm, tn):
    t, dm = x.shape
    return pl.pallas_call(
        _out_proj_kernel,
        grid=(t // tm, dm // tn),
        in_specs=[
            pl.BlockSpec((tm, dm), lambda i, j: (i, 0)),
            pl.BlockSpec((dm, tn), lambda i, j: (0, j)),
            pl.BlockSpec((tm, tn), lambda i, j: (i, j)),
        ],
        out_specs=pl.BlockSpec((tm, tn), lambda i, j: (i, j)),
        out_shape=jax.ShapeDtypeStruct((t, dm), F32),
        compiler_params=_cparams("parallel", "arbitrary"),
        name="out_proj",
    )(merged, w, x)


def _top16(s, tm):
    key = lax.broadcasted_iota(jnp.int32, s.shape, 0).astype(F32)
    work = s
    rank = jnp.full(s.shape, float(PEER_TOPK), F32)
    vals = []
    for a in range(PEER_TOPK):
        m = jnp.max(work, axis=0, keepdims=True)
        idx = jnp.min(jnp.where(work == m, key, float(PEER_N_KEYS)), axis=0, keepdims=True)
        sel = key == idx
        rank = jnp.where(sel, float(a), rank)
        work = jnp.where(sel, -jnp.inf, work)
        vals.append(m)
    return vals, rank


def _peer_route_kernel(q_ref, k1_ref, k2_ref, e1_ref, e2_ref, r2_ref, nrow_ref, *, tm, heads):
    k1 = k1_ref[...].astype(BF16)
    k2 = k2_ref[...].astype(BF16)
    for hh in range(heads):
        q = q_ref[:, hh * PEER_DK:(hh + 1) * PEER_DK]
        e1_ref[hh], e2_ref[hh], r2_ref[hh], nrow_ref[hh] = _route_head(q, k1, k2, tm)


def _route_head(q, k1, k2, tm):
    half = PEER_DK // 2
    s1 = _dot_nt(k1, q[:, :half].astype(BF16))
    s2 = _dot_nt(k2, q[:, half:].astype(BF16))
    t1, rank1 = _top16(s1, tm)
    t2, rank2 = _top16(s2, tm)

    k = PEER_TOPK
    cands = [(a, b) for a in range(k) for b in range(k) if (a + 1) * (b + 1) <= k]
    no_pos = float(k * k)
    work, pos = [], []
    for g in range(0, len(cands), SUBLANES):
        group = cands[g:g + SUBLANES]
        pad = SUBLANES - len(group)
        work.append(jnp.concatenate([t1[a] + t2[b] for a, b in group]
                                    + [jnp.full((1, tm), -jnp.inf, F32)] * pad, axis=0))
        pos.append(jnp.concatenate([jnp.full((1, tm), float(a * k + b), F32) for a, b in group]
                                   + [jnp.full((1, tm), no_pos, F32)] * pad, axis=0))
    row_id = lax.broadcasted_iota(jnp.int32, (k, tm), 0).astype(F32)
    n_won = jnp.zeros((k, tm), F32)
    tops = []
    for _ in range(k):
        m = functools.reduce(jnp.maximum, work)
        m = jnp.max(m, axis=0, keepdims=True)
        p = functools.reduce(jnp.minimum, [jnp.where(w_ == m, p_, no_pos) for w_, p_ in zip(work, pos)])
        p = jnp.min(p, axis=0, keepdims=True)
        work = [jnp.where(p_ == p, -jnp.inf, w_) for w_, p_ in zip(work, pos)]
        n_won = n_won + (row_id == jnp.floor(p * (1.0 / k))).astype(F32)
        tops.append(m)
    zsum = jnp.zeros((1, tm), F32)
    for m in tops:
        zsum = zsum + jnp.exp(m - tops[0])
    nrow = jnp.zeros(s1.shape, F32)
    for a in range(k):
        nrow = jnp.where(rank1 == float(a), n_won[a:a + 1, :], nrow)

    return jnp.exp(s1 - t1[0]) / zsum, jnp.exp(s2 - t2[0]), rank2, nrow


def _peer_route(q, k1, k2, *, tm):
    t = q.shape[0]
    nk = PEER_N_KEYS
    out = jax.ShapeDtypeStruct((PEER_HEADS, nk, t), F32)
    heads = 8
    spec = pl.BlockSpec((heads, nk, tm), lambda i, h: (h, 0, i))
    return pl.pallas_call(
        functools.partial(_peer_route_kernel, tm=tm, heads=heads),
        grid=(t // tm, PEER_HEADS // heads),
        in_specs=[
            pl.BlockSpec((tm, heads * PEER_DK), lambda i, h: (i, h)),
            pl.BlockSpec((nk, PEER_DK // 2), lambda i, h: (0, 0)),
            pl.BlockSpec((nk, PEER_DK // 2), lambda i, h: (0, 0)),
        ],
        out_specs=[spec, spec, spec, spec],
        out_shape=[out, out, out, out],
        compiler_params=_cparams("parallel", "arbitrary"),
        name="peer_route",
    )(q, k1, k2)


def _gelu_tanh(x):
    c = 0.7978845608028654
    return x * (0.5 + 0.5 * jnp.tanh(x * (c + (c * 0.044715) * (x * x))))


def _peer_dense_kernel(x_ref, gn_ref, gf_ref, e1_ref, nrow_ref, e2_ref, r2_ref, u_ref, v_ref, y_ref,
                       h_ref, s_ref, p_ref, *, tm, eb, nj):
    j = pl.program_id(1)
    nk = PEER_N_KEYS

    @pl.when(j == 0)
    def _():
        h_ref[...] = _rms(x_ref[...], gn_ref[...]).astype(BF16)
        y_ref[...] = jnp.zeros_like(y_ref)

    s_ref[...] = _dot_nt(u_ref[...], h_ref[...])
    for bi in range(eb):
        for th in range(tm // LANES):
            ts = slice(th * LANES, (th + 1) * LANES)
            w = jnp.zeros((nk, LANES), F32)
            for h in range(PEER_HEADS):
                sel = r2_ref[h, :, ts] < nrow_ref[h, bi:bi + 1, ts]
                w = jnp.where(sel, w + e1_ref[h, bi:bi + 1, ts] * e2_ref[h, :, ts], w)
            act = _gelu_tanh(s_ref[bi * nk:(bi + 1) * nk, ts])
            p_ref[ts, bi * nk:(bi + 1) * nk] = (w * act).T.astype(BF16)
    y_ref[...] += _dot(p_ref[...], v_ref[...])

    @pl.when(j == nj - 1)
    def _():
        y_ref[...] = _rms(x_ref[...] + y_ref[...], gf_ref[...])


def _peer_dense(x, g_ffn, g_final, e1, e2, r2, nrow, u16, v16, *, tm, eb):
    t, dm = x.shape
    nk = PEER_N_KEYS
    ne = u16.shape[0]
    nj = ne // (eb * nk)
    return pl.pallas_call(
        functools.partial(_peer_dense_kernel, tm=tm, eb=eb, nj=nj),
        grid=(t // tm, nj),
        in_specs=[
            pl.BlockSpec((tm, dm), lambda i, j: (i, 0)),
            pl.BlockSpec((1, dm), lambda i, j: (0, 0)),
            pl.BlockSpec((1, dm), lambda i, j: (0, 0)),
            pl.BlockSpec((PEER_HEADS, eb, tm), lambda i, j: (0, j, i)),
            pl.BlockSpec((PEER_HEADS, eb, tm), lambda i, j: (0, j, i)),
            pl.BlockSpec((PEER_HEADS, nk, tm), lambda i, j: (0, 0, i)),
            pl.BlockSpec((PEER_HEADS, nk, tm), lambda i, j: (0, 0, i)),
            pl.BlockSpec((eb * nk, dm), lambda i, j: (j, 0)),
            pl.BlockSpec((eb * nk, dm), lambda i, j: (j, 0)),
        ],
        out_specs=pl.BlockSpec((tm, dm), lambda i, j: (i, 0)),
        out_shape=jax.ShapeDtypeStruct((t, dm), F32),
        scratch_shapes=[
            pltpu.VMEM((tm, dm), BF16),
            pltpu.VMEM((eb * nk, tm), F32),
            pltpu.VMEM((tm, eb * nk), BF16),
        ],
        compiler_params=_cparams("parallel", "arbitrary"),
        name="peer_dense",
    )(x, g_ffn.reshape(1, dm), g_final.reshape(1, dm), e1, nrow, e2, r2, u16, v16)


def _arrange_kernel(w_ref, o_ref, *, pieces, used):
    for src, size, dst in pieces:
        o_ref[dst:dst + size, :] = w_ref[src:src + size, :].astype(BF16)
    o_ref[used:, :] = jnp.zeros((o_ref.shape[0] - used, o_ref.shape[1]), BF16)


def _arrange_w_in(w_in):
    d, n_in = w_in.shape
    w_t = jnp.transpose(w_in)
    sizes = (CONV_DIM, CONV_DIM, 3 * DN_DIM, DN_DIM, DN_HEADS, DN_HEADS, XA_DIM, 2048, 2048, 2048)
    offs = [0]
    for s in sizes:
        offs.append(offs[-1] + s)
    pieces = ((offs[0], offs[2] - offs[0], COL_GLU), (offs[2], offs[3] - offs[2], COL_QKV),
              (offs[3], offs[4] - offs[3], COL_Z), (offs[4], offs[6] - offs[4], COL_AB),
              (offs[6], offs[7] - offs[6], COL_XQ), (offs[7], offs[10] - offs[7], COL_GATES))
    return pl.pallas_call(
        functools.partial(_arrange_kernel, pieces=pieces, used=COL_AB + 2 * DN_HEADS),
        grid=(d // LANES,),
        in_specs=[pl.BlockSpec((n_in, LANES), lambda i: (0, i))],
        out_specs=pl.BlockSpec((PROJ_COLS, LANES), lambda i: (0, i)),
        out_shape=jax.ShapeDtypeStruct((PROJ_COLS, d), BF16),
        compiler_params=_cparams("parallel"),
        name="arrange_w_in",
    )(w_t)


def _layer(x, mem_k, mem_v, conv_buf, qkv_buf, delta_state, w, *, chunk, tiles, g_final):
    bsz, seq, dm = x.shape
    t = bsz * seq
    assert seq % chunk == 0
    tiles = {name: min(size, t) for name, size in tiles.items()}
    x2d = x.reshape(t, dm)
    proj = _norm_matmul(x2d, w["norm_mix"], w["w_in"], tm=tiles["tm_proj"], tn=PROJ_TN, name="in_proj",
                        w_is_transposed=True)
    c, new_conv = _conv_branch(proj, conv_buf, w["conv_dw"], w["conv_dw_b"], w["conv_ln_g"], w["conv_ln_b"],
                               bsz=bsz, seq=seq, tt=tiles["tt_conv"], bb=min(tiles["seqs_per_step"], bsz))
    qkvc, g, beta, new_qkv = _dn_prep(proj, qkv_buf, w["dn_conv_w"], w["dn_a_log"], w["dn_dt_bias"],
                                      bsz=bsz, seq=seq, tt=tiles["tt_dn"], bb=min(tiles["seqs_per_step"], bsz))
    o, new_state = _dn_chunk(qkvc, proj, g, beta, delta_state, w["dn_norm_g"], bsz=bsz, seq=seq, c=chunk,
                             bb=2 if seq == chunk and bsz % 2 == 0 else 1)
    if mem_k.ndim == 4:
        a = _cached_attention(proj, mem_k, mem_v, bsz=bsz, seq=seq, bb=2 if bsz % 2 == 0 else 1)
    else:
        a = _mem_attention(proj, mem_k, mem_v, bsz=bsz, seq=seq, tl=tiles["tl_attn"])
    merged = _merge(c, o, a, proj, w["w_conv_out"], w["w_dn_out"], w["w_xa_out"], tm=tiles["tm_mix"], tn=512)
    x2 = _out_proj(merged, w["w_out"], x2d, tm=tiles["tm_merge"], tn=1024)
    q = _norm_matmul(x2, w["norm_ffn"], w["w_peer_q"], tm=tiles["tm_merge"], tn=1024, name="peer_q")
    e1, e2, r2, nrow = _peer_route(q, w["peer_keys_1"], w["peer_keys_2"], tm=LANES)
    y = _peer_dense(x2, w["norm_ffn"], g_final, e1, e2, r2, nrow, w["peer_u"], w["peer_v"],
                    tm=tiles["tm_peer"], eb=8)
    return y.reshape(bsz, seq, dm), new_conv, new_qkv, new_state


PROMPT_TILES = dict(tm_proj=1024, tt_conv=256, tt_dn=128, seqs_per_step=1, tl_attn=256, tm_mix=1024, tm_merge=1024,
                    tm_peer=512)
SAMPLE_TILES = dict(tm_proj=1024, tt_conv=8, tt_dn=8, seqs_per_step=8, tl_attn=8, tm_mix=512, tm_merge=1024,
                    tm_peer=512)


def kernel(x_prompt, x_sample, mem_prompt, cache_mem_k, cache_mem_v, state_conv, state_qkv_conv, state_delta, norm_mix, norm_mem, w_in, conv_dw, conv_dw_b, conv_ln_g, conv_ln_b, w_conv_out, dn_conv_w, dn_a_log, dn_dt_bias, dn_norm_g, w_dn_out, w_mem_kv, w_xa_out, w_out, norm_ffn, w_peer_q, peer_keys_1, peer_keys_2, peer_u, peer_v, norm_final):
    depth = w_in.shape[0]
    assert depth == 1, "the final norm is fused into the last layer's PEER kernel; one layer is supported"
    bp, sp, dm = x_prompt.shape
    bs, ss, _ = x_sample.shape
    n_mem = mem_prompt.shape[1]
    def layer0(a):
        return a.reshape(a.shape[1:])

    w = dict(
        norm_mix=layer0(norm_mix), w_in=_arrange_w_in(layer0(w_in)), conv_dw=layer0(conv_dw),
        conv_dw_b=layer0(conv_dw_b), conv_ln_g=layer0(conv_ln_g), conv_ln_b=layer0(conv_ln_b),
        w_conv_out=layer0(w_conv_out).astype(BF16), dn_conv_w=layer0(dn_conv_w), dn_a_log=layer0(dn_a_log),
        dn_dt_bias=layer0(dn_dt_bias), dn_norm_g=layer0(dn_norm_g), w_dn_out=layer0(w_dn_out).astype(BF16),
        w_xa_out=layer0(w_xa_out).astype(BF16), w_out=layer0(w_out).astype(BF16), norm_ffn=layer0(norm_ffn),
        w_peer_q=layer0(w_peer_q).astype(BF16), peer_keys_1=layer0(peer_keys_1), peer_keys_2=layer0(peer_keys_2),
        peer_u=layer0(peer_u).astype(BF16), peer_v=layer0(peer_v).astype(BF16),
    )
    mem2d = mem_prompt.reshape(bp * n_mem, dm)
    wkv = layer0(w_mem_kv).astype(BF16)
    mk = _norm_matmul(mem2d, layer0(norm_mem), wkv[:, :XA_DIM], tm=bp * n_mem, tn=512, name="mem_k")
    mv = _norm_matmul(mem2d, layer0(norm_mem), wkv[:, XA_DIM:], tm=bp * n_mem, tn=512, name="mem_v")
    mk = mk.reshape(bp, n_mem, XA_DIM)
    mv = mv.reshape(bp, n_mem, XA_DIM)
    yp, conv_p, qkv_p, delta_p = _layer(
        x_prompt, mk, mv,
        jnp.zeros((bp, CONV_K - 1, CONV_DIM), F32),
        jnp.zeros((bp, DN_CONV_K - 1, 3 * DN_DIM), F32),
        jnp.zeros((bp, DN_HEADS, DN_HEAD_DIM, DN_HEAD_DIM), F32),
        w, chunk=min(DN_CHUNK, sp), tiles=PROMPT_TILES, g_final=norm_final)
    ys, conv_s, qkv_s, delta_s = _layer(
        x_sample, layer0(cache_mem_k), layer0(cache_mem_v),
        layer0(state_conv), layer0(state_qkv_conv), layer0(state_delta),
        w, chunk=min(DN_CHUNK, ss), tiles=SAMPLE_TILES, g_final=norm_final)
    kv_shape = (1, bp, n_mem, XA_HEADS, XA_HEAD_DIM)
    return (yp, ys, mk.reshape(kv_shape), mv.reshape(kv_shape), conv_p[None], qkv_p[None], delta_p[None],
            conv_s[None], qkv_s[None], delta_s[None])
```
